```python
import jax
import jax.numpy as jnp
from jax import lax
import numpy as np

D_MODEL = 2048
BATCH = 4
SEQ = 2048
DEPTH = 4
DEC_BATCH = 32
DEC_SEQ = 1
PAST_LEN = 16384
PAGE_SIZE = 128

N_A_LAYERS = DEPTH // 2
N_B_LAYERS = DEPTH - N_A_LAYERS
HD_A = 128
H_A = D_MODEL // HD_A
KVH_A = 4
G_A = H_A // KVH_A
Q_BLOCK = 128
SB_BIAS_INIT = -7.0
HD_B = 64
H_B = D_MODEL // HD_B
KVH_B = 4
G_B = H_B // KVH_B
WINDOW = 128
SWA_BLOCK = WINDOW
ROPE_DIM = HD_B // 4
ROPE_THETA = 500000.0
D_FF = ((8 * D_MODEL // 3 + 255) // 256) * 256
CONV_W = 3
EPS = 1e-6

kernel_name = 'yoco_stickbreak_swa_sink_convffn_step'


def rms_norm(x, g):
    xf = x.astype(jnp.float32)
    y = xf * lax.rsqrt(jnp.mean(xf * xf, axis=-1, keepdims=True) + EPS)
    return (y * g.astype(jnp.float32)).astype(x.dtype)


def partial_rope(x, pos):
    half = ROPE_DIM // 2
    inv_freq = 1.0 / (ROPE_THETA ** (jnp.arange(half, dtype=jnp.float32) * (2.0 / ROPE_DIM)))
    ang = pos.astype(jnp.float32)[:, None] * inv_freq[None, :]
    cos = jnp.cos(ang)[None, :, None, :]
    sin = jnp.sin(ang)[None, :, None, :]
    xr = x[..., :ROPE_DIM].astype(jnp.float32)
    x1, x2 = xr[..., :half], xr[..., half:]
    rot = jnp.concatenate([x1 * cos - x2 * sin, x2 * cos + x1 * sin], axis=-1).astype(x.dtype)
    return jnp.concatenate([rot, x[..., ROPE_DIM:]], axis=-1)


def stick_breaking(q, k, v, q_pos, k_pos, bias):
    z = jnp.einsum('bqkgd,bskd->bkgqs', q, k, preferred_element_type=jnp.float32) * (HD_A ** -0.5)
    z = z + bias.astype(jnp.float32).reshape(KVH_A, G_A)[None, :, :, None, None]
    valid = k_pos[None, :] < q_pos[:, None]
    log_skip = jnp.where(valid, jax.nn.log_sigmoid(-z), 0.0)
    later = lax.cumsum(log_skip, axis=4, reverse=True) - log_skip
    w = jnp.where(valid, jnp.exp(jax.nn.log_sigmoid(z) + later), 0.0)
    return jnp.einsum('bkgqs,bskd->bqkgd', w, v.astype(jnp.float32))


def blocked_stick_breaking(q, k, v, bias):
    B, T = q.shape[:2]
    nb = T // Q_BLOCK
    pos = jnp.arange(T)
    qb = jnp.moveaxis(q.reshape(B, nb, Q_BLOCK, KVH_A, G_A, HD_A), 1, 0)
    pb = pos.reshape(nb, Q_BLOCK)
    ob = lax.map(lambda a: stick_breaking(a[0], k, v, a[1], pos, bias), (qb, pb))
    return jnp.moveaxis(ob, 0, 1).reshape(B, T, KVH_A, G_A, HD_A)


def window_mask(q_pos, k_pos):
    return (k_pos <= q_pos) & (k_pos >= q_pos - WINDOW) & (k_pos >= 0)


def sink_attention(q, k, v, mask, sinks):
    s = jnp.einsum('nqkgd,nskd->nkgqs', q, k, preferred_element_type=jnp.float32) * (HD_B ** -0.5)
    s = jnp.where(mask[:, None, None], s, -jnp.inf)
    sink = sinks.astype(jnp.float32).reshape(KVH_B, G_B)[None, :, :, None, None]
    m = jnp.maximum(jnp.max(s, axis=-1, keepdims=True), sink)
    p = jnp.exp(s - m)
    p = p / (jnp.sum(p, axis=-1, keepdims=True) + jnp.exp(sink - m))
    return jnp.einsum('nkgqs,nskd->nqkgd', p, v.astype(jnp.float32))


def banded_sink_attention(q, k, v, sinks):
    B, T = q.shape[:2]
    nb = T // SWA_BLOCK

    def with_prev(a):
        prev = jnp.concatenate([jnp.zeros_like(a[:, :SWA_BLOCK]), a[:, :T - SWA_BLOCK]], axis=1)
        prev = prev.reshape(B, nb, SWA_BLOCK, KVH_B, HD_B)
        cur = a.reshape(B, nb, SWA_BLOCK, KVH_B, HD_B)
        return jnp.concatenate([prev, cur], axis=2).reshape(B * nb, 2 * SWA_BLOCK, KVH_B, HD_B)

    kb, vb = with_prev(k), with_prev(v)
    qb = q.reshape(B * nb, SWA_BLOCK, KVH_B, G_B, HD_B)
    q_pos = jnp.arange(T).reshape(nb, SWA_BLOCK)
    k_pos = jnp.concatenate([q_pos - SWA_BLOCK, q_pos], axis=1)
    mask = window_mask(q_pos[:, :, None], k_pos[:, None, :])
    mask = jnp.broadcast_to(mask[None], (B, nb, SWA_BLOCK, 2 * SWA_BLOCK)).reshape(B * nb, SWA_BLOCK, 2 * SWA_BLOCK)
    o = sink_attention(qb, kb, vb, mask, sinks)
    return o.reshape(B, T, KVH_B, G_B, HD_B)


def conv_ffn(h, prev, w_up, conv_w, conv_b, w_down):
    T = h.shape[1]
    u = h @ w_up
    upad = jnp.concatenate([prev.astype(u.dtype), u], axis=1)
    c = conv_b + conv_w[0] * upad[:, 0:T]
    for i in range(1, CONV_W):
        c = c + conv_w[i] * upad[:, i:i + T]
    gate, val = c[..., :D_FF], c[..., D_FF:]
    return (jax.nn.gelu(gate, approximate=True) * val) @ w_down, upad[:, -(CONV_W - 1):]


def _trunk(x, past_len, cache_a_k, cache_a_v, page_table, win_k_buf, win_v_buf, conv_buf,
           norm_g, w_qkv_a, w_o_a, sb_bias, w_kv_b, kv_norm_g, w_q_b, w_o_b, sinks_b,
           w_up, conv_w, conv_b, w_down):
    B, T, _ = x.shape
    sample = cache_a_k is not None
    q_pos = past_len + jnp.arange(T)
    new_ak, new_av, new_conv = [], [], []
    for l in range(DEPTH):
        h = rms_norm(x, norm_g[l, 0])
        if l < N_A_LAYERS:
            qkv = h @ w_qkv_a[l]
            q = qkv[..., :H_A * HD_A].reshape(B, T, KVH_A, G_A, HD_A)
            k = qkv[..., H_A * HD_A:(H_A + KVH_A) * HD_A].reshape(B, T, KVH_A, HD_A)
            v = qkv[..., (H_A + KVH_A) * HD_A:].reshape(B, T, KVH_A, HD_A)
            new_ak.append(k)
            new_av.append(v)
            if sample:
                k_past = cache_a_k[l][page_table].reshape(B, past_len, KVH_A, HD_A)
                v_past = cache_a_v[l][page_table].reshape(B, past_len, KVH_A, HD_A)
                k_all = jnp.concatenate([k_past.astype(k.dtype), k], axis=1)
                v_all = jnp.concatenate([v_past.astype(v.dtype), v], axis=1)
                o = stick_breaking(q, k_all, v_all, q_pos, jnp.arange(past_len + T), sb_bias[l])
            else:
                o = blocked_stick_breaking(q, k, v, sb_bias[l])
            mix = o.reshape(B, T, H_A * HD_A).astype(x.dtype) @ w_o_a[l]
        else:
            if l == N_A_LAYERS:
                kv = rms_norm(x, kv_norm_g) @ w_kv_b
                ks = partial_rope(kv[..., :KVH_B * HD_B].reshape(B, T, KVH_B, HD_B), q_pos)
                vs = kv[..., KVH_B * HD_B:].reshape(B, T, KVH_B, HD_B)
                if sample:
                    ks_all = jnp.concatenate([win_k_buf.astype(ks.dtype), ks], axis=1)
                    vs_all = jnp.concatenate([win_v_buf.astype(vs.dtype), vs], axis=1)
                    k_pos = past_len - win_k_buf.shape[1] + jnp.arange(ks_all.shape[1])
                    swa_mask = window_mask(q_pos[:, None], k_pos[None, :])[None]
                else:
                    ks_all, vs_all = ks, vs
                n_keep = min(WINDOW, ks_all.shape[1])
                new_wk = ks_all[:, -n_keep:]
                new_wv = vs_all[:, -n_keep:]
            j = l - N_A_LAYERS
            qb = partial_rope((h @ w_q_b[j]).reshape(B, T, H_B, HD_B), q_pos).reshape(B, T, KVH_B, G_B, HD_B)
            if sample:
                o = sink_attention(qb, ks_all, vs_all, swa_mask, sinks_b[j])
            else:
                o = banded_sink_attention(qb, ks, vs, sinks_b[j])
            mix = o.reshape(B, T, H_B * HD_B).astype(x.dtype) @ w_o_b[j]
        x = x + rms_norm(mix, norm_g[l, 1])
        h = rms_norm(x, norm_g[l, 2])
        prev = conv_buf[l] if sample else jnp.zeros((B, CONV_W - 1, 2 * D_FF), x.dtype)
        f, conv_rows = conv_ffn(h, prev, w_up[l], conv_w[l], conv_b[l], w_down[l])
        new_conv.append(conv_rows)
        x = x + rms_norm(f, norm_g[l, 3])
    return x, jnp.stack(new_ak), jnp.stack(new_av), new_wk, new_wv, jnp.stack(new_conv)


def setup_inputs(seed: int = 0) -> dict:
    key = jax.random.key(seed)
    ks = jax.random.split(key, 24)
    n_pages = PAST_LEN // PAGE_SIZE
    n_pool = (DEC_BATCH * n_pages * 5) // 4
    win_buf = min(WINDOW, PAST_LEN)

    def nrm(k, shape, scale=1.0):
        return jax.random.normal(k, shape, jnp.float32) * scale

    page_table = jax.random.permutation(ks[7], n_pool)[:DEC_BATCH * n_pages].reshape(DEC_BATCH, n_pages).astype(jnp.int32)
    return {
        'x_prompt': nrm(ks[0], (BATCH, SEQ, D_MODEL)),
        'x_sample': nrm(ks[1], (DEC_BATCH, DEC_SEQ, D_MODEL)),
        'cache_a_k': nrm(ks[2], (N_A_LAYERS, n_pool, PAGE_SIZE, KVH_A, HD_A)),
        'cache_a_v': nrm(ks[3], (N_A_LAYERS, n_pool, PAGE_SIZE, KVH_A, HD_A)),
        'state_win_k': nrm(ks[4], (DEC_BATCH, win_buf, KVH_B, HD_B)),
        'state_win_v': nrm(ks[5], (DEC_BATCH, win_buf, KVH_B, HD_B)),
        'state_conv': nrm(ks[6], (DEPTH, DEC_BATCH, CONV_W - 1, 2 * D_FF)),
        'page_table': page_table,
        'norm_g': 1.0 + nrm(ks[8], (DEPTH, 4, D_MODEL), 0.05),
        'w_qkv_a': nrm(ks[9], (N_A_LAYERS, D_MODEL, (H_A + 2 * KVH_A) * HD_A), D_MODEL ** -0.5),
        'w_o_a': nrm(ks[10], (N_A_LAYERS, H_A * HD_A, D_MODEL), (H_A * HD_A) ** -0.5),
        'sb_bias': SB_BIAS_INIT + nrm(ks[20], (N_A_LAYERS, H_A), 0.3),
        'w_kv_b': nrm(ks[11], (D_MODEL, 2 * KVH_B * HD_B), D_MODEL ** -0.5),
        'kv_norm_g': 1.0 + nrm(ks[12], (D_MODEL,), 0.05),
        'w_q_b': nrm(ks[13], (N_B_LAYERS, D_MODEL, H_B * HD_B), D_MODEL ** -0.5),
        'w_o_b': nrm(ks[14], (N_B_LAYERS, H_B * HD_B, D_MODEL), (H_B * HD_B) ** -0.5),
        'sinks_b': nrm(ks[15], (N_B_LAYERS, H_B), 0.5),
        'w_up': nrm(ks[16], (DEPTH, D_MODEL, 2 * D_FF), D_MODEL ** -0.5),
        'conv_w': nrm(ks[17], (DEPTH, CONV_W, 2 * D_FF), CONV_W ** -0.5),
        'conv_b': nrm(ks[18], (DEPTH, 2 * D_FF), 0.02),
        'w_down': nrm(ks[19], (DEPTH, D_FF, D_MODEL), D_FF ** -0.5),
    }


def reference(x_prompt, x_sample, cache_a_k, cache_a_v, state_win_k, state_win_v, state_conv, page_table,
              norm_g, w_qkv_a, w_o_a, sb_bias, w_kv_b, kv_norm_g, w_q_b, w_o_b, sinks_b, w_up, conv_w, conv_b, w_down):
    weights = (norm_g, w_qkv_a, w_o_a, sb_bias, w_kv_b, kv_norm_g, w_q_b, w_o_b, sinks_b, w_up, conv_w, conv_b, w_down)
    y_prompt, ak_p, av_p, wk_p, wv_p, conv_p = _trunk(
        x_prompt, 0, None, None, None, None, None, None, *weights)
    past_len = page_table.shape[1] * PAGE_SIZE
    y_sample, ak_s, av_s, wk_s, wv_s, conv_s = _trunk(
        x_sample, past_len, cache_a_k, cache_a_v, page_table, state_win_k, state_win_v, state_conv, *weights)
    return (y_prompt, y_sample, ak_p, av_p, ak_s, av_s, wk_p, wv_p, wk_s, wv_s, conv_p, conv_s)
```

```python
import functools

import jax
import jax.numpy as jnp
from jax import lax
from jax.experimental import pallas as pl
from jax.experimental.pallas import tpu as pltpu

F32 = jnp.float32
BF16 = jnp.bfloat16

EPS = 1e-6
PAGE_SIZE = 128
HD_A = 128
KVH_A = 4
HD_B = 64
KVH_B = 4
WINDOW = 128
ROPE_DIM = HD_B // 4
ROPE_THETA = 500000.0
CONV_W = 3

LANES = 128
SUBLANES = 8
VMEM_LIMIT = 56 * 1024 * 1024

_NT = (((1,), (1,)), ((), ()))


def _params(n_axes, vmem=VMEM_LIMIT):
    return pltpu.CompilerParams(dimension_semantics=("arbitrary",) * n_axes, vmem_limit_bytes=vmem)


def _rms(x, g):
    ms = jnp.mean(x * x, axis=-1, keepdims=True)
    return x * lax.rsqrt(ms + EPS) * g


def _norm_matmul_kernel(*refs, n_rope_tiles, n_col_tiles):
    if n_rope_tiles:
        x_ref, g_ref, w_ref, c_ref, sa_ref, sb_ref, o_ref, h_ref = refs
    else:
        x_ref, g_ref, w_ref, o_ref, h_ref = refs
    j = pl.program_id(1)

    @pl.when(j == 0)
    def _():
        h_ref[...] = _rms(x_ref[...], g_ref[...]).astype(BF16)

    acc = jnp.dot(h_ref[...], w_ref[...], preferred_element_type=F32)

    def plain():
        o_ref[...] = acc

    def roped():
        c, sa, sb = c_ref[...], sa_ref[...], sb_ref[...]
        for t in range(acc.shape[1] // LANES):
            a = acc[:, t * LANES:(t + 1) * LANES]
            o_ref[:, t * LANES:(t + 1) * LANES] = (
                a * c + pltpu.roll(a, LANES - ROPE_DIM // 2, 1) * sa + pltpu.roll(a, ROPE_DIM // 2, 1) * sb)

    if n_rope_tiles == 0:
        plain()
    elif n_rope_tiles >= n_col_tiles:
        roped()
    else:
        pl.when(j < n_rope_tiles)(roped)
        pl.when(j >= n_rope_tiles)(plain)


def norm_matmul(x, g, w, *, tm, tn, rope=None, n_rope_tiles=0, name):
    m, d = x.shape
    n = w.shape[1]
    n_col_tiles = n // tn
    in_specs = [
        pl.BlockSpec((tm, d), lambda i, j: (i, 0)),
        pl.BlockSpec((1, d), lambda i, j: (0, 0)),
        pl.BlockSpec((d, tn), lambda i, j: (0, j)),
    ]
    args = [x, g.reshape(1, d), w]
    if n_rope_tiles:
        n_tab = rope[0].shape[0] // tm
        for t in rope:
            in_specs.append(pl.BlockSpec((tm, LANES), lambda i, j: (i % n_tab, 0)))
            args.append(t)
    return pl.pallas_call(
        functools.partial(_norm_matmul_kernel, n_rope_tiles=n_rope_tiles, n_col_tiles=n_col_tiles),
        grid=(m // tm, n_col_tiles),
        in_specs=in_specs,
        out_specs=pl.BlockSpec((tm, tn), lambda i, j: (i, j)),
        out_shape=jax.ShapeDtypeStruct((m, n), F32),
        scratch_shapes=[pltpu.VMEM((tm, d), BF16)],
        compiler_params=_params(2),
        name=name,
    )(*args)


def _matmul_norm_res_kernel(a_ref, w_ref, g_ref, x_ref, o_ref):
    m = jnp.dot(a_ref[...].astype(BF16), w_ref[...], preferred_element_type=F32)
    o_ref[...] = x_ref[...] + _rms(m, g_ref[...])


def matmul_norm_res(a, w, g, x, *, tm, name):
    m, k = a.shape
    n = w.shape[1]
    return pl.pallas_call(
        _matmul_norm_res_kernel,
        grid=(m // tm,),
        in_specs=[
            pl.BlockSpec((tm, k), lambda i: (i, 0)),
            pl.BlockSpec((k, n), lambda i: (0, 0)),
            pl.BlockSpec((1, n), lambda i: (0, 0)),
            pl.BlockSpec((tm, n), lambda i: (i, 0)),
        ],
        out_specs=pl.BlockSpec((tm, n), lambda i: (i, 0)),
        out_shape=jax.ShapeDtypeStruct((m, n), F32),
        compiler_params=_params(1),
        name=name,
    )(a, w, g.reshape(1, n), x)


def _ffn_kernel(x_ref, g2_ref, wg_ref, wv_ref, cwg_ref, cwv_ref, cbg_ref, cbv_ref, wd_ref, g3_ref,
                o_ref, cog_ref, cov_ref,
                h_ref, acc_ref, carg_ref, carv_ref, bufg_ref, bufv_ref, *, tiles_per_seq):
    i = pl.program_id(0)
    j = pl.program_id(1)
    tm = x_ref.shape[0]

    @pl.when(j == 0)
    def _():
        h_ref[...] = _rms(x_ref[...], g2_ref[...]).astype(BF16)
        acc_ref[...] = jnp.zeros_like(acc_ref)

    seq_start = (i % tiles_per_seq) == 0
    h = h_ref[...]

    def conv(w_ref, cw_ref, cb_ref, car_ref, buf_ref, co_ref):
        u = jnp.dot(h, w_ref[...], preferred_element_type=F32)

        @pl.when(seq_start)
        def _():
            buf_ref[0:SUBLANES, :] = jnp.zeros((SUBLANES, u.shape[1]), F32)

        @pl.when(jnp.logical_not(seq_start))
        def _():
            buf_ref[0:SUBLANES, :] = car_ref[j]

        buf_ref[SUBLANES:SUBLANES + tm, :] = u
        tail = u[tm - SUBLANES:tm, :]
        car_ref[j] = tail
        co_ref[0] = tail
        cw = cw_ref[...]
        c = cb_ref[...] + cw[0:1] * buf_ref[SUBLANES - 2:SUBLANES - 2 + tm, :]
        c = c + cw[1:2] * buf_ref[SUBLANES - 1:SUBLANES - 1 + tm, :]
        return c + cw[2:3] * u

    cg = conv(wg_ref, cwg_ref, cbg_ref, carg_ref, bufg_ref, cog_ref)
    cv = conv(wv_ref, cwv_ref, cbv_ref, carv_ref, bufv_ref, cov_ref)
    act = (jax.nn.gelu(cg, approximate=True) * cv).astype(BF16)
    acc_ref[...] += jnp.dot(act, wd_ref[...], preferred_element_type=F32)

    @pl.when(j == pl.num_programs(1) - 1)
    def _():
        o_ref[...] = x_ref[...] + _rms(acc_ref[...], g3_ref[...])


def ffn(x, g2, w_up, conv_w, conv_b, w_down, g3, *, seq_len, tm, tf, name):
    m, d = x.shape
    d_ff = w_down.shape[0]
    nj = d_ff // tf
    n_seq = m // seq_len
    tiles_per_seq = seq_len // tm
    conv_b2 = conv_b.reshape(1, 2 * d_ff)
    out, cog, cov = pl.pallas_call(
        functools.partial(_ffn_kernel, tiles_per_seq=tiles_per_seq),
        grid=(m // tm, nj),
        in_specs=[
            pl.BlockSpec((tm, d), lambda i, j: (i, 0)),
            pl.BlockSpec((1, d), lambda i, j: (0, 0)),
            pl.BlockSpec((d, tf), lambda i, j: (0, j)),
            pl.BlockSpec((d, tf), lambda i, j: (0, j + nj)),
            pl.BlockSpec((CONV_W, tf), lambda i, j: (0, j)),
            pl.BlockSpec((CONV_W, tf), lambda i, j: (0, j + nj)),
            pl.BlockSpec((1, tf), lambda i, j: (0, j)),
            pl.BlockSpec((1, tf), lambda i, j: (0, j + nj)),
            pl.BlockSpec((tf, d), lambda i, j: (j, 0)),
            pl.BlockSpec((1, d), lambda i, j: (0, 0)),
        ],
        out_specs=[
            pl.BlockSpec((tm, d), lambda i, j: (i, 0)),
            pl.BlockSpec((1, SUBLANES, tf), lambda i, j: (i // tiles_per_seq, 0, j)),
            pl.BlockSpec((1, SUBLANES, tf), lambda i, j: (i // tiles_per_seq, 0, j)),
        ],
        out_shape=[
            jax.ShapeDtypeStruct((m, d), F32),
            jax.ShapeDtypeStruct((n_seq, SUBLANES, d_ff), F32),
            jax.ShapeDtypeStruct((n_seq, SUBLANES, d_ff), F32),
        ],
        scratch_shapes=[
            pltpu.VMEM((tm, d), BF16),
            pltpu.VMEM((tm, d), F32),
            pltpu.VMEM((nj, SUBLANES, tf), F32),
            pltpu.VMEM((nj, SUBLANES, tf), F32),
            pltpu.VMEM((tm + SUBLANES, tf), F32),
            pltpu.VMEM((tm + SUBLANES, tf), F32),
        ],
        compiler_params=_params(2),
        name=name,
    )(x, g2.reshape(1, d), w_up, w_up, conv_w, conv_w, conv_b2, conv_b2, w_down, g3.reshape(1, d))
    return out, cog, cov


def _ffn_step_kernel(x_ref, g2_ref, wg_ref, wv_ref, cwg_ref, cwv_ref, cbg_ref, cbv_ref,
                     p0g_ref, p0v_ref, p1g_ref, p1v_ref, wd_ref, g3_ref,
                     o_ref, ug_ref, uv_ref, h_ref, acc_ref):
    j = pl.program_id(0)

    @pl.when(j == 0)
    def _():
        h_ref[...] = _rms(x_ref[...], g2_ref[...]).astype(BF16)
        acc_ref[...] = jnp.zeros_like(acc_ref)

    h = h_ref[...]

    def conv(w_ref, cw_ref, cb_ref, p0_ref, p1_ref, u_ref):
        u = jnp.dot(h, w_ref[...], preferred_element_type=F32)
        u_ref[...] = u
        cw = cw_ref[...]
        c = cb_ref[...] + cw[0:1] * p0_ref[...]
        c = c + cw[1:2] * p1_ref[...]
        return c + cw[2:3] * u

    cg = conv(wg_ref, cwg_ref, cbg_ref, p0g_ref, p1g_ref, ug_ref)
    cv = conv(wv_ref, cwv_ref, cbv_ref, p0v_ref, p1v_ref, uv_ref)
    act = (jax.nn.gelu(cg, approximate=True) * cv).astype(BF16)
    acc_ref[...] += jnp.dot(act, wd_ref[...], preferred_element_type=F32)

    @pl.when(j == pl.num_programs(0) - 1)
    def _():
        o_ref[...] = x_ref[...] + _rms(acc_ref[...], g3_ref[...])


def ffn_step(x, g2, w_up, conv_w, conv_b, prev0, prev1, w_down, g3, *, tf, name):
    m, d = x.shape
    d_ff = w_down.shape[0]
    nj = d_ff // tf
    conv_b2 = conv_b.reshape(1, 2 * d_ff)
    lo = lambda j: (0, j)
    hi = lambda j: (0, j + nj)
    out, ug, uv = pl.pallas_call(
        _ffn_step_kernel,
        grid=(nj,),
        in_specs=[
            pl.BlockSpec((m, d), lambda j: (0, 0)),
            pl.BlockSpec((1, d), lambda j: (0, 0)),
            pl.BlockSpec((d, tf), lo),
            pl.BlockSpec((d, tf), hi),
            pl.BlockSpec((CONV_W, tf), lo),
            pl.BlockSpec((CONV_W, tf), hi),
            pl.BlockSpec((1, tf), lo),
            pl.BlockSpec((1, tf), hi),
            pl.BlockSpec((m, tf), lo),
            pl.BlockSpec((m, tf), hi),
            pl.BlockSpec((m, tf), lo),
            pl.BlockSpec((m, tf), hi),
            pl.BlockSpec((tf, d), lambda j: (j, 0)),
            pl.BlockSpec((1, d), lambda j: (0, 0)),
        ],
        out_specs=[
            pl.BlockSpec((m, d), lambda j: (0, 0)),
            pl.BlockSpec((m, tf), lo),
            pl.BlockSpec((m, tf), lo),
        ],
        out_shape=[
            jax.ShapeDtypeStruct((m, d), F32),
            jax.ShapeDtypeStruct((m, d_ff), F32),
            jax.ShapeDtypeStruct((m, d_ff), F32),
        ],
        scratch_shapes=[pltpu.VMEM((m, d), BF16), pltpu.VMEM((m, d), F32)],
        compiler_params=_params(1),
        name=name,
    )(x, g2.reshape(1, d), w_up, w_up, conv_w, conv_w, conv_b2, conv_b2,
      prev0, prev0, prev1, prev1, w_down, g3.reshape(1, d))
    return out, jnp.concatenate([ug, uv], axis=-1)


def _suffix_sum_matrix():
    j = lax.broadcasted_iota(jnp.int32, (LANES, 2 * LANES), 0)
    s = lax.broadcasted_iota(jnp.int32, (LANES, 2 * LANES), 1)
    return jnp.where((s >= LANES) | (j > s), 1.0, 0.0).astype(BF16)


def _sb_block(z, valid, w2, car, vb):
    sp = jnp.maximum(z, 0.0) + jnp.log1p(jnp.exp(-jnp.abs(z)))
    log_skip = -sp
    log_beta = z - sp
    if valid is not None:
        log_skip = jnp.where(valid, log_skip, 0.0)
    hi = log_skip.astype(BF16)
    lo = (log_skip - hi.astype(F32)).astype(BF16)
    cum = jnp.dot(hi, w2, preferred_element_type=F32) + jnp.dot(lo, w2, preferred_element_type=F32)
    w = jnp.exp(log_beta + cum[:, :LANES] + car)
    if valid is not None:
        w = jnp.where(valid, w, 0.0)
    return jnp.dot(w.astype(BF16), vb, preferred_element_type=F32), car + cum[:, LANES:]


def _sb_prompt_kernel(bias_ref, q_ref, k_ref, v_ref, w2_ref, o_ref, acc_ref, car_ref, *, scale, group):
    kvh = pl.program_id(1)
    i = pl.program_id(2)
    tq = q_ref.shape[0]
    q = q_ref[...]
    qs = jnp.concatenate([q[:, g * HD_A:(g + 1) * HD_A] for g in range(group)], axis=0).astype(BF16)
    w2 = w2_ref[...]
    acc_ref[...] = jnp.zeros_like(acc_ref)
    car_ref[...] = jnp.zeros_like(car_ref)
    row = lax.broadcasted_iota(jnp.int32, (tq, tq), 0)
    col = lax.broadcasted_iota(jnp.int32, (tq, tq), 1)
    causal = col < row

    def block(j, valid):
        start = pl.multiple_of(j * tq, tq)
        kb = k_ref[pl.ds(start, tq), :].astype(BF16)
        vb = v_ref[pl.ds(start, tq), :].astype(BF16)
        z_all = lax.dot_general(qs, kb, _NT, preferred_element_type=F32)
        for g in range(group):
            rows = slice(g * tq, (g + 1) * tq)
            z = z_all[rows] * scale + bias_ref[kvh * group + g]
            pv, car = _sb_block(z, valid, w2, car_ref[rows], vb)
            car_ref[rows] = car
            acc_ref[rows] += pv

    block(i, causal)

    def body(jj, c):
        block(i - 1 - jj, None)
        return c

    lax.fori_loop(0, i, body, 0)
    for g in range(group):
        o_ref[:, g * HD_A:(g + 1) * HD_A] = acc_ref[g * tq:(g + 1) * tq].astype(o_ref.dtype)


def sb_prompt(qkv, bias, *, n_seq, seq_len, name):
    m = qkv.shape[0]
    h_a = bias.shape[0]
    group = h_a // KVH_A
    tq = PAGE_SIZE
    nq = seq_len // tq
    qw = group * HD_A
    k_col0 = h_a * HD_A // HD_A
    v_col0 = k_col0 + KVH_A
    return pl.pallas_call(
        functools.partial(_sb_prompt_kernel, scale=HD_A ** -0.5, group=group),
        grid=(n_seq, KVH_A, nq),
        in_specs=[
            pl.BlockSpec(memory_space=pltpu.SMEM),
            pl.BlockSpec((tq, qw), lambda b, h, i: (b * nq + i, h)),
            pl.BlockSpec((seq_len, HD_A), lambda b, h, i: (b, k_col0 + h)),
            pl.BlockSpec((seq_len, HD_A), lambda b, h, i: (b, v_col0 + h)),
            pl.BlockSpec((LANES, 2 * LANES), lambda b, h, i: (0, 0)),
        ],
        out_specs=pl.BlockSpec((tq, qw), lambda b, h, i: (b * nq + i, h)),
        out_shape=jax.ShapeDtypeStruct((m, h_a * HD_A), BF16),
        scratch_shapes=[pltpu.VMEM((group * tq, HD_A), F32), pltpu.VMEM((group * tq, LANES), F32)],
        compiler_params=_params(3),
        name=name,
    )(bias, qkv, qkv, qkv, _suffix_sum_matrix())


def _sb_decode_kernel(pt_ref, q_ref, bias_ref, w2_ref, *refs, scale, pps, group):
    k_refs = refs[:pps]
    v_refs = refs[pps:2 * pps]
    o_ref, acc_ref, car_ref = refs[2 * pps:]
    s = pl.program_id(1)

    @pl.when(s == 0)
    def _():
        acc_ref[...] = jnp.zeros_like(acc_ref)
        car_ref[...] = jnp.zeros_like(car_ref)

    q = q_ref[0]
    row_kvh = lax.broadcasted_iota(jnp.int32, q.shape, 0) // group
    qbd = jnp.concatenate([jnp.where(row_kvh == c, q, 0.0) for c in range(KVH_A)], axis=1).astype(BF16)
    w2 = w2_ref[...]
    bias = bias_ref[...]
    for r in range(pps):
        kp = k_refs[r][...].astype(BF16)
        vp = v_refs[r][...].astype(BF16)
        z = lax.dot_general(qbd, kp, _NT, preferred_element_type=F32) * scale + bias
        pv, car = _sb_block(z, None, w2, car_ref[...], vp)
        car_ref[...] = car
        acc_ref[...] += pv

    @pl.when(s == pl.num_programs(1) - 1)
    def _():
        acc = acc_ref[...]
        o = jnp.zeros(q.shape, F32)
        for c in range(KVH_A):
            o = o + jnp.where(row_kvh == c, acc[:, c * HD_A:(c + 1) * HD_A], 0.0)
        o_ref[0] = o


def sb_decode(q, bias, cache_k, cache_v, page_table, *, layer, pps, name):
    n, h_a, _ = q.shape
    group = h_a // KVH_A
    n_pages = page_table.shape[1]
    kvw = KVH_A * HD_A

    def page_spec(r):
        def index_map(b, s, pt):
            return (layer, pt[b * n_pages + (n_pages - 1 - (s * pps + r))], 0, 0)
        return pl.BlockSpec((None, None, PAGE_SIZE, kvw), index_map)

    grid_spec = pltpu.PrefetchScalarGridSpec(
        num_scalar_prefetch=1,
        grid=(n, n_pages // pps),
        in_specs=[
            pl.BlockSpec((1, h_a, HD_A), lambda b, s, pt: (b, 0, 0)),
            pl.BlockSpec((h_a, LANES), lambda b, s, pt: (0, 0)),
            pl.BlockSpec((LANES, 2 * LANES), lambda b, s, pt: (0, 0)),
        ] + [page_spec(r) for r in range(pps)] * 2,
        out_specs=pl.BlockSpec((1, h_a, HD_A), lambda b, s, pt: (b, 0, 0)),
        scratch_shapes=[pltpu.VMEM((h_a, kvw), F32), pltpu.VMEM((h_a, LANES), F32)],
    )
    bias_b = jnp.broadcast_to(bias.astype(F32)[:, None], (h_a, LANES))
    return pl.pallas_call(
        functools.partial(_sb_decode_kernel, scale=HD_A ** -0.5, pps=pps, group=group),
        grid_spec=grid_spec,
        out_shape=jax.ShapeDtypeStruct((n, h_a, HD_A), F32),
        compiler_params=_params(2),
        name=name,
    )(page_table.reshape(-1), q, bias_b, _suffix_sum_matrix(), *([cache_k] * pps), *([cache_v] * pps))


def _swa_prompt_kernel(sink_ref, q_ref, kc_ref, kp_ref, vc_ref, vp_ref, o_ref, *, scale, n_heads):
    i = pl.program_id(1)
    tq = q_ref.shape[0]
    half = HD_B
    lane = lax.broadcasted_iota(jnp.int32, (tq, LANES), 1)
    low = lane < half
    row = lax.broadcasted_iota(jnp.int32, (tq, tq), 0)
    col = lax.broadcasted_iota(jnp.int32, (tq, tq), 1)
    mask_cur = col <= row
    mask_prev = (col >= row) & (i > 0)
    mask = jnp.concatenate([mask_prev, mask_cur], axis=1)
    mask2 = jnp.concatenate([mask, mask], axis=0)
    low2 = jnp.concatenate([low, low], axis=0)
    group = n_heads // KVH_B
    for c in range(KVH_B // 2):
        kk = jnp.concatenate([kp_ref[:, c * LANES:(c + 1) * LANES], kc_ref[:, c * LANES:(c + 1) * LANES]], axis=0)
        vv = jnp.concatenate([vp_ref[:, c * LANES:(c + 1) * LANES], vc_ref[:, c * LANES:(c + 1) * LANES]], axis=0)
        kr = pltpu.roll(kk, half, 1)
        vr = pltpu.roll(vv, half, 1)
        for e in range(2):
            kvh = 2 * c + e
            kd = (jnp.where(low2, kk, kr) if e == 0 else jnp.where(low2, kr, kk)).astype(BF16)
            vd = (jnp.where(low2, vv, vr) if e == 0 else jnp.where(low2, vr, vv)).astype(BF16)
            for p in range(group // 2):
                h0 = kvh * group + 2 * p
                qcol = h0 // 2
                qp = q_ref[:, qcol * LANES:(qcol + 1) * LANES]
                q2 = jnp.concatenate([jnp.where(low, qp, 0.0), jnp.where(low, 0.0, qp)], axis=0).astype(BF16)
                s = lax.dot_general(q2, kd, _NT, preferred_element_type=F32) * scale
                s = jnp.where(mask2, s, -jnp.inf)
                rid = lax.broadcasted_iota(jnp.int32, (2 * tq, 1), 0)
                sink = jnp.where(rid < tq, sink_ref[h0], sink_ref[h0 + 1])
                m = jnp.maximum(jnp.max(s, axis=-1, keepdims=True), sink)
                pexp = jnp.exp(s - m)
                denom = jnp.sum(pexp, axis=-1, keepdims=True) + jnp.exp(sink - m)
                o2 = jnp.dot(pexp.astype(BF16), vd, preferred_element_type=F32) / denom
                o_ref[:, qcol * LANES:(qcol + 1) * LANES] = jnp.where(low, o2[:tq], o2[tq:]).astype(o_ref.dtype)


def swa_prompt(q, kv, sinks, *, n_seq, seq_len, name):
    m, qw = q.shape
    n_heads = sinks.shape[0]
    tq = WINDOW
    nq = seq_len // tq
    kw = KVH_B * HD_B
    cur = lambda col: (lambda b, i: (b * nq + i, col))
    prev = lambda col: (lambda b, i: (b * nq + jnp.maximum(i - 1, 0), col))
    return pl.pallas_call(
        functools.partial(_swa_prompt_kernel, scale=HD_B ** -0.5, n_heads=n_heads),
        grid=(n_seq, nq),
        in_specs=[
            pl.BlockSpec(memory_space=pltpu.SMEM),
            pl.BlockSpec((tq, qw), lambda b, i: (b * nq + i, 0)),
            pl.BlockSpec((tq, kw), cur(0)),
            pl.BlockSpec((tq, kw), prev(0)),
            pl.BlockSpec((tq, kw), cur(1)),
            pl.BlockSpec((tq, kw), prev(1)),
        ],
        out_specs=pl.BlockSpec((tq, qw), lambda b, i: (b * nq + i, 0)),
        out_shape=jax.ShapeDtypeStruct((m, qw), BF16),
        compiler_params=_params(2),
        name=name,
    )(sinks, q, kv, kv, kv, kv)


def _swa_decode_kernel(q_ref, sink_ref, kw_ref, vw_ref, kn_ref, vn_ref, o_ref, *, scale, group):
    q = q_ref[0]
    row_kvh = lax.broadcasted_iota(jnp.int32, q.shape, 0) // group
    lane_kvh = lax.broadcasted_iota(jnp.int32, q.shape, 1) // HD_B
    own = row_kvh == lane_kvh
    qbd = jnp.where(own, q, 0.0)
    qb = qbd.astype(BF16)
    kwin = kw_ref[0].astype(BF16)
    vwin = vw_ref[0].astype(BF16)
    kn = kn_ref[0]
    vn = vn_ref[0]
    s = lax.dot_general(qb, kwin, _NT, preferred_element_type=F32) * scale
    s_new = jnp.sum(qb.astype(F32) * kn.astype(BF16).astype(F32), axis=-1, keepdims=True) * scale
    sink = sink_ref[:, 0:1]
    m = jnp.maximum(jnp.maximum(jnp.max(s, axis=-1, keepdims=True), s_new), sink)
    p = jnp.exp(s - m)
    p_new = jnp.exp(s_new - m)
    denom = jnp.sum(p, axis=-1, keepdims=True) + p_new + jnp.exp(sink - m)
    o = jnp.dot(p.astype(BF16), vwin, preferred_element_type=F32)
    o = o + p_new.astype(BF16).astype(F32) * vn.astype(BF16).astype(F32)
    o = jnp.where(own, o / denom, 0.0)
    o = o + pltpu.roll(o, 2 * HD_B, 1)
    o = o + pltpu.roll(o, HD_B, 1)
    o_ref[0] = o[:, :HD_B]


def swa_decode(q, sinks, win_k, win_v, k_new, v_new, *, name):
    n, n_heads, _ = q.shape
    w = win_k.shape[1]
    kw = KVH_B * HD_B
    q_t = jnp.tile(q, (1, 1, KVH_B))
    sink_b = jnp.broadcast_to(sinks.astype(F32)[:, None], (n_heads, LANES))
    per_seq = lambda a, b: pl.BlockSpec((1, a, b), lambda s: (s, 0, 0))
    return pl.pallas_call(
        functools.partial(_swa_decode_kernel, scale=HD_B ** -0.5, group=n_heads // KVH_B),
        grid=(n,),
        in_specs=[
            per_seq(n_heads, kw),
            pl.BlockSpec((n_heads, LANES), lambda s: (0, 0)),
            per_seq(w, kw), per_seq(w, kw), per_seq(1, kw), per_seq(1, kw),
        ],
        out_specs=per_seq(n_heads, HD_B),
        out_shape=jax.ShapeDtypeStruct((n, n_heads, HD_B), F32),
        compiler_params=_params(1),
        name=name,
    )(q_t, sink_b, win_k, win_v, k_new, v_new)


def _rope_tables(pos):
    n = pos.shape[0]
    half = ROPE_DIM // 2
    inv_freq = 1.0 / (ROPE_THETA ** (jnp.arange(half, dtype=F32) * (2.0 / ROPE_DIM)))
    ang = pos.astype(F32)[:, None] * inv_freq[None, :]
    cos, sin = jnp.cos(ang), jnp.sin(ang)
    pad = lambda a, left, fill: jnp.concatenate(
        [jnp.full((n, left), fill, F32), a, jnp.full((n, HD_B - left - a.shape[1]), fill, F32)], axis=1)
    c = jnp.concatenate([cos, cos, jnp.ones((n, HD_B - ROPE_DIM), F32)], axis=1)
    sa = pad(-sin, 0, 0.0)
    sb = pad(sin, half, 0.0)
    two = lambda a: jnp.concatenate([a, a], axis=1)
    return two(c), two(sa), two(sb)


def kernel(x_prompt, x_sample, cache_a_k, cache_a_v, state_win_k, state_win_v, state_conv, page_table, norm_g, w_qkv_a, w_o_a, sb_bias, w_kv_b, kv_norm_g, w_q_b, w_o_b, sinks_b, w_up, conv_w, conv_b, w_down):
    n_seq, seq_len, d = x_prompt.shape
    n_dec = x_sample.shape[0]
    depth = norm_g.shape[0]
    n_a = w_qkv_a.shape[0]
    h_a = sb_bias.shape[1]
    h_b = sinks_b.shape[1]
    d_ff = w_down.shape[1]
    past_len = page_table.shape[1] * PAGE_SIZE
    qa_w = h_a * HD_A
    ka_w = KVH_A * HD_A
    kb_w = KVH_B * HD_B

    bf = lambda a: a.astype(BF16)
    w_qkv_a, w_o_a, w_kv_b, w_q_b, w_o_b, w_up, w_down = map(bf, (w_qkv_a, w_o_a, w_kv_b, w_q_b, w_o_b, w_up, w_down))

    rope_p = _rope_tables(jnp.arange(seq_len))
    rope_s = _rope_tables(jnp.full((n_dec,), past_len))
    cache_k = cache_a_k.reshape(cache_a_k.shape[:3] + (ka_w,))
    cache_v = cache_a_v.reshape(cache_a_v.shape[:3] + (ka_w,))
    win_k = state_win_k.reshape(n_dec, -1, kb_w)
    win_v = state_win_v.reshape(n_dec, -1, kb_w)

    x = x_prompt.reshape(n_seq * seq_len, d)
    ak_p, av_p, conv_p = [], [], []
    kv = None
    for l in range(depth):
        if l < n_a:
            qkv = norm_matmul(x, norm_g[l, 0], w_qkv_a[l], tm=512, tn=512, name=f"p{l}_qkv")
            ak_p.append(qkv[:, qa_w:qa_w + ka_w].reshape(n_seq, seq_len, KVH_A, HD_A))
            av_p.append(qkv[:, qa_w + ka_w:].reshape(n_seq, seq_len, KVH_A, HD_A))
            o = sb_prompt(qkv, sb_bias[l], n_seq=n_seq, seq_len=seq_len, name=f"p{l}_sb")
            x = matmul_norm_res(o, w_o_a[l], norm_g[l, 1], x, tm=256, name=f"p{l}_wo")
        else:
            jb = l - n_a
            if kv is None:
                kv = norm_matmul(x, kv_norm_g, w_kv_b, tm=512, tn=kb_w, rope=rope_p, n_rope_tiles=1, name="p_kv")
            qb = norm_matmul(x, norm_g[l, 0], w_q_b[jb], tm=512, tn=512, rope=rope_p,
                             n_rope_tiles=h_b * HD_B // 512, name=f"p{l}_q")
            o = swa_prompt(qb, kv, sinks_b[jb], n_seq=n_seq, seq_len=seq_len, name=f"p{l}_swa")
            x = matmul_norm_res(o, w_o_b[jb], norm_g[l, 1], x, tm=256, name=f"p{l}_wo")
        x, cog, cov = ffn(x, norm_g[l, 2], w_up[l], conv_w[l], conv_b[l], w_down[l], norm_g[l, 3],
                          seq_len=seq_len, tm=512, tf=512, name=f"p{l}_ffn")
        conv_p.append(jnp.concatenate([cog[:, SUBLANES - 2:], cov[:, SUBLANES - 2:]], axis=-1))
    y_prompt = x.reshape(n_seq, seq_len, d)
    kv_p = kv.reshape(n_seq, seq_len, 2, KVH_B, HD_B)
    n_keep = min(WINDOW, seq_len)
    wk_p = kv_p[:, seq_len - n_keep:, 0]
    wv_p = kv_p[:, seq_len - n_keep:, 1]

    x = x_sample.reshape(n_dec, d)
    ak_s, av_s, conv_s = [], [], []
    kv = None
    for l in range(depth):
        if l < n_a:
            qkv = norm_matmul(x, norm_g[l, 0], w_qkv_a[l], tm=n_dec, tn=512, name=f"s{l}_qkv")
            ak_s.append(qkv[:, qa_w:qa_w + ka_w].reshape(n_dec, 1, KVH_A, HD_A))
            av_s.append(qkv[:, qa_w + ka_w:].reshape(n_dec, 1, KVH_A, HD_A))
            o = sb_decode(qkv[:, :qa_w].reshape(n_dec, h_a, HD_A), sb_bias[l], cache_k, cache_v, page_table,
                          layer=l, pps=8, name=f"s{l}_sb")
            x = matmul_norm_res(o.reshape(n_dec, qa_w), w_o_a[l], norm_g[l, 1], x, tm=n_dec, name=f"s{l}_wo")
        else:
            jb = l - n_a
            if kv is None:
                kv = norm_matmul(x, kv_norm_g, w_kv_b, tm=n_dec, tn=kb_w, rope=rope_s, n_rope_tiles=1, name="s_kv")
                k_new = kv[:, None, :kb_w]
                v_new = kv[:, None, kb_w:]
            qb = norm_matmul(x, norm_g[l, 0], w_q_b[jb], tm=n_dec, tn=512, rope=rope_s,
                             n_rope_tiles=h_b * HD_B // 512, name=f"s{l}_q")
            o = swa_decode(qb.reshape(n_dec, h_b, HD_B), sinks_b[jb], win_k, win_v, k_new, v_new, name=f"s{l}_swa")
            x = matmul_norm_res(o.reshape(n_dec, h_b * HD_B), w_o_b[jb], norm_g[l, 1], x, tm=n_dec, name=f"s{l}_wo")
        x, u_new = ffn_step(x, norm_g[l, 2], w_up[l], conv_w[l], conv_b[l],
                            state_conv[l, :, 0], state_conv[l, :, 1], w_down[l], norm_g[l, 3],
                            tf=512, name=f"s{l}_ffn")
        conv_s.append(jnp.concatenate([state_conv[l, :, 1:], u_new[:, None, :]], axis=1)[:, -(CONV_W - 1):])
    y_sample = x.reshape(n_dec, 1, d)
    n_keep_s = min(WINDOW, win_k.shape[1] + 1)
    wk_s = jnp.concatenate([win_k, k_new], axis=1)[:, -n_keep_s:].reshape(n_dec, n_keep_s, KVH_B, HD_B)
    wv_s = jnp.concatenate([win_v, v_new], axis=1)[:, -n_keep_s:].reshape(n_dec, n_keep_s, KVH_B, HD_B)

    return (y_prompt, y_sample, jnp.stack(ak_p), jnp.stack(av_p), jnp.stack(ak_s), jnp.stack(av_s),
            wk_p, wv_p, wk_s, wv_s, jnp.stack(conv_p), jnp.stack(conv_s))
```

```python
import functools

import jax
import jax.numpy as jnp
from jax import lax
from jax.experimental import pallas as pl
from jax.experimental.pallas import tpu as pltpu

F32 = jnp.float32
BF16 = jnp.bfloat16

EPS = 1e-6
PAGE_SIZE = 128
HD_A = 128
KVH_A = 4
HD_B = 64
KVH_B = 4
WINDOW = 128
ROPE_DIM = HD_B // 4
ROPE_THETA = 500000.0
CONV_W = 3

LANES = 128
SUBLANES = 8
VMEM_LIMIT = 56 * 1024 * 1024
FFN_TM = 512

_NT = (((1,), (1,)), ((), ()))


def _params(n_axes, vmem=VMEM_LIMIT):
    return pltpu.CompilerParams(dimension_semantics=("arbitrary",) * n_axes, vmem_limit_bytes=vmem)


def _rms(x, g):
    ms = jnp.mean(x * x, axis=-1, keepdims=True)
    return x * lax.rsqrt(ms + EPS) * g


def _norm_matmul_kernel(*refs, n_rope_tiles, n_col_tiles):
    if n_rope_tiles:
        x_ref, g_ref, w_ref, c_ref, sa_ref, sb_ref, o_ref, h_ref = refs
    else:
        x_ref, g_ref, w_ref, o_ref, h_ref = refs
    j = pl.program_id(1)

    @pl.when(j == 0)
    def _():
        h_ref[...] = _rms(x_ref[...], g_ref[...]).astype(BF16)

    acc = jnp.dot(h_ref[...], w_ref[...], preferred_element_type=F32)

    def plain():
        o_ref[...] = acc

    def roped():
        c, sa, sb = c_ref[...], sa_ref[...], sb_ref[...]
        for t in range(acc.shape[1] // LANES):
            a = acc[:, t * LANES:(t + 1) * LANES]
            o_ref[:, t * LANES:(t + 1) * LANES] = (
                a * c + pltpu.roll(a, LANES - ROPE_DIM // 2, 1) * sa + pltpu.roll(a, ROPE_DIM // 2, 1) * sb)

    if n_rope_tiles == 0:
        plain()
    elif n_rope_tiles >= n_col_tiles:
        roped()
    else:
        pl.when(j < n_rope_tiles)(roped)
        pl.when(j >= n_rope_tiles)(plain)


def norm_matmul(x, g, w, *, tm, tn, rope=None, n_rope_tiles=0, name):
    m, d = x.shape
    n = w.shape[1]
    n_col_tiles = n // tn
    in_specs = [
        pl.BlockSpec((tm, d), lambda i, j: (i, 0)),
        pl.BlockSpec((1, d), lambda i, j: (0, 0)),
        pl.BlockSpec((d, tn), lambda i, j: (0, j)),
    ]
    args = [x, g.reshape(1, d), w]
    if n_rope_tiles:
        n_tab = rope[0].shape[0] // tm
        for t in rope:
            in_specs.append(pl.BlockSpec((tm, LANES), lambda i, j: (i % n_tab, 0)))
            args.append(t)
    return pl.pallas_call(
        functools.partial(_norm_matmul_kernel, n_rope_tiles=n_rope_tiles, n_col_tiles=n_col_tiles),
        grid=(m // tm, n_col_tiles),
        in_specs=in_specs,
        out_specs=pl.BlockSpec((tm, tn), lambda i, j: (i, j)),
        out_shape=jax.ShapeDtypeStruct((m, n), F32),
        scratch_shapes=[pltpu.VMEM((tm, d), BF16)],
        compiler_params=_params(2),
        name=name,
    )(*args)


def _matmul_norm_res_kernel(a_ref, w_ref, g_ref, x_ref, o_ref):
    m = jnp.dot(a_ref[...].astype(BF16), w_ref[...], preferred_element_type=F32)
    o_ref[...] = x_ref[...] + _rms(m, g_ref[...])


def matmul_norm_res(a, w, g, x, *, tm, name):
    m, k = a.shape
    n = w.shape[1]
    return pl.pallas_call(
        _matmul_norm_res_kernel,
        grid=(m // tm,),
        in_specs=[
            pl.BlockSpec((tm, k), lambda i: (i, 0)),
            pl.BlockSpec((k, n), lambda i: (0, 0)),
            pl.BlockSpec((1, n), lambda i: (0, 0)),
            pl.BlockSpec((tm, n), lambda i: (i, 0)),
        ],
        out_specs=pl.BlockSpec((tm, n), lambda i: (i, 0)),
        out_shape=jax.ShapeDtypeStruct((m, n), F32),
        compiler_params=_params(1),
        name=name,
    )(a, w, g.reshape(1, n), x)


def _ffn_kernel(x_ref, g2_ref, wg_ref, wv_ref, cwg_ref, cwv_ref, cbg_ref, cbv_ref, wd_ref, g3_ref,
                o_ref, cog_ref, cov_ref,
                h_ref, acc_ref, carg_ref, carv_ref, *buf_refs, tiles_per_seq):
    i = pl.program_id(0)
    j = pl.program_id(1)
    tm = x_ref.shape[0]
    n_chunks = len(buf_refs) // 2
    bufg_refs, bufv_refs = buf_refs[:n_chunks], buf_refs[n_chunks:]
    chunk = bufg_refs[0].shape[1]
    col = lambda c: slice(c * chunk, (c + 1) * chunk)

    @pl.when(j == 0)
    def _():
        h_ref[...] = _rms(x_ref[...], g2_ref[...]).astype(BF16)
        acc_ref[...] = jnp.zeros_like(acc_ref)

    seq_start = (i % tiles_per_seq) == 0

    @pl.when(seq_start)
    def _():
        for buf_ref in buf_refs:
            buf_ref[0:SUBLANES, :] = jnp.zeros((SUBLANES, chunk), F32)

    @pl.when(jnp.logical_not(seq_start))
    def _():
        for c in range(n_chunks):
            bufg_refs[c][0:SUBLANES, :] = carg_ref[j, :, col(c)]
            bufv_refs[c][0:SUBLANES, :] = carv_ref[j, :, col(c)]

    for c in range(n_chunks):
        bufg_refs[c][SUBLANES:SUBLANES + tm, :] = jnp.dot(h_ref[...], wg_ref[:, col(c)], preferred_element_type=F32)
        bufv_refs[c][SUBLANES:SUBLANES + tm, :] = jnp.dot(h_ref[...], wv_ref[:, col(c)], preferred_element_type=F32)

    def conv(c, cw_ref, cb_ref, buf_ref, car_ref, co_ref):
        tail = buf_ref[tm:tm + SUBLANES, :]
        car_ref[j, :, col(c)] = tail
        co_ref[0, :, col(c)] = tail
        cw = cw_ref[:, col(c)]
        y = cb_ref[:, col(c)] + cw[0:1] * buf_ref[SUBLANES - 2:SUBLANES - 2 + tm, :]
        y = y + cw[1:2] * buf_ref[SUBLANES - 1:SUBLANES - 1 + tm, :]
        return y + cw[2:3] * buf_ref[SUBLANES:SUBLANES + tm, :]

    for c in range(n_chunks):
        cg = conv(c, cwg_ref, cbg_ref, bufg_refs[c], carg_ref, cog_ref)
        cv = conv(c, cwv_ref, cbv_ref, bufv_refs[c], carv_ref, cov_ref)
        act = (jax.nn.gelu(cg, approximate=True) * cv).astype(BF16)
        acc_ref[...] += jnp.dot(act, wd_ref[col(c), :], preferred_element_type=F32)

    @pl.when(j == pl.num_programs(1) - 1)
    def _():
        o_ref[...] = x_ref[...] + _rms(acc_ref[...], g3_ref[...])


def ffn(x, g2, w_up, conv_w, conv_b, w_down, g3, *, seq_len, tm, tf, chunk, name):
    m, d = x.shape
    d_ff = w_down.shape[0]
    nj = d_ff // tf
    tiles_per_seq = seq_len // tm
    conv_b2 = conv_b.reshape(1, 2 * d_ff)
    out, cog, cov = pl.pallas_call(
        functools.partial(_ffn_kernel, tiles_per_seq=tiles_per_seq),
        grid=(m // tm, nj),
        in_specs=[
            pl.BlockSpec((tm, d), lambda i, j: (i, 0)),
            pl.BlockSpec((1, d), lambda i, j: (0, 0)),
            pl.BlockSpec((d, tf), lambda i, j: (0, j)),
            pl.BlockSpec((d, tf), lambda i, j: (0, j + nj)),
            pl.BlockSpec((CONV_W, tf), lambda i, j: (0, j)),
            pl.BlockSpec((CONV_W, tf), lambda i, j: (0, j + nj)),
            pl.BlockSpec((1, tf), lambda i, j: (0, j)),
            pl.BlockSpec((1, tf), lambda i, j: (0, j + nj)),
            pl.BlockSpec((tf, d), lambda i, j: (j, 0)),
            pl.BlockSpec((1, d), lambda i, j: (0, 0)),
        ],
        out_specs=[
            pl.BlockSpec((tm, d), lambda i, j: (i, 0)),
            pl.BlockSpec((1, SUBLANES, tf), lambda i, j: (i, 0, j)),
            pl.BlockSpec((1, SUBLANES, tf), lambda i, j: (i, 0, j)),
        ],
        out_shape=[
            jax.ShapeDtypeStruct((m, d), F32),
            jax.ShapeDtypeStruct((m // tm, SUBLANES, d_ff), F32),
            jax.ShapeDtypeStruct((m // tm, SUBLANES, d_ff), F32),
        ],
        scratch_shapes=[
            pltpu.VMEM((tm, d), BF16),
            pltpu.VMEM((tm, d), F32),
            pltpu.VMEM((nj, SUBLANES, tf), F32),
            pltpu.VMEM((nj, SUBLANES, tf), F32),
        ] + [pltpu.VMEM((tm + SUBLANES, chunk), F32)] * (2 * (tf // chunk)),
        compiler_params=_params(2),
        name=name,
    )(x, g2.reshape(1, d), w_up, w_up, conv_w, conv_w, conv_b2, conv_b2, w_down, g3.reshape(1, d))
    return out, cog, cov


def _ffn_step_kernel(x_ref, g2_ref, wg_ref, wv_ref, cwg_ref, cwv_ref, cbg_ref, cbv_ref,
                     p0g_ref, p0v_ref, p1g_ref, p1v_ref, wd_ref, g3_ref,
                     o_ref, ug_ref, uv_ref, h_ref, acc_ref):
    j = pl.program_id(0)

    @pl.when(j == 0)
    def _():
        h_ref[...] = _rms(x_ref[...], g2_ref[...]).astype(BF16)
        acc_ref[...] = jnp.zeros_like(acc_ref)

    h = h_ref[...]

    def conv(w_ref, cw_ref, cb_ref, p0_ref, p1_ref, u_ref):
        u = jnp.dot(h, w_ref[...], preferred_element_type=F32)
        u_ref[...] = u
        cw = cw_ref[...]
        c = cb_ref[...] + cw[0:1] * p0_ref[...]
        c = c + cw[1:2] * p1_ref[...]
        return c + cw[2:3] * u

    cg = conv(wg_ref, cwg_ref, cbg_ref, p0g_ref, p1g_ref, ug_ref)
    cv = conv(wv_ref, cwv_ref, cbv_ref, p0v_ref, p1v_ref, uv_ref)
    act = (jax.nn.gelu(cg, approximate=True) * cv).astype(BF16)
    acc_ref[...] += jnp.dot(act, wd_ref[...], preferred_element_type=F32)

    @pl.when(j == pl.num_programs(0) - 1)
    def _():
        o_ref[...] = x_ref[...] + _rms(acc_ref[...], g3_ref[...])


def ffn_step(x, g2, w_up, conv_w, conv_b, prev0, prev1, w_down, g3, *, tf, name):
    m, d = x.shape
    d_ff = w_down.shape[0]
    nj = d_ff // tf
    conv_b2 = conv_b.reshape(1, 2 * d_ff)
    lo = lambda j: (0, j)
    hi = lambda j: (0, j + nj)
    out, ug, uv = pl.pallas_call(
        _ffn_step_kernel,
        grid=(nj,),
        in_specs=[
            pl.BlockSpec((m, d), lambda j: (0, 0)),
            pl.BlockSpec((1, d), lambda j: (0, 0)),
            pl.BlockSpec((d, tf), lo),
            pl.BlockSpec((d, tf), hi),
            pl.BlockSpec((CONV_W, tf), lo),
            pl.BlockSpec((CONV_W, tf), hi),
            pl.BlockSpec((1, tf), lo),
            pl.BlockSpec((1, tf), hi),
            pl.BlockSpec((m, tf), lo),
            pl.BlockSpec((m, tf), hi),
            pl.BlockSpec((m, tf), lo),
            pl.BlockSpec((m, tf), hi),
            pl.BlockSpec((tf, d), lambda j: (j, 0)),
            pl.BlockSpec((1, d), lambda j: (0, 0)),
        ],
        out_specs=[
            pl.BlockSpec((m, d), lambda j: (0, 0)),
            pl.BlockSpec((m, tf), lo),
            pl.BlockSpec((m, tf), lo),
        ],
        out_shape=[
            jax.ShapeDtypeStruct((m, d), F32),
            jax.ShapeDtypeStruct((m, d_ff), F32),
            jax.ShapeDtypeStruct((m, d_ff), F32),
        ],
        scratch_shapes=[pltpu.VMEM((m, d), BF16), pltpu.VMEM((m, d), F32)],
        compiler_params=_params(1),
        name=name,
    )(x, g2.reshape(1, d), w_up, w_up, conv_w, conv_w, conv_b2, conv_b2,
      prev0, prev0, prev1, prev1, w_down, g3.reshape(1, d))
    return out, jnp.concatenate([ug, uv], axis=-1)


def _suffix_sum_matrix():
    j = lax.broadcasted_iota(jnp.int32, (2 * LANES, 2 * LANES), 0) % LANES
    s = lax.broadcasted_iota(jnp.int32, (2 * LANES, 2 * LANES), 1)
    return jnp.where((s >= LANES) | (j > s), 1.0, 0.0).astype(BF16)


def _sb_terms(z, valid):
    sp = jnp.maximum(z, 0.0) + jnp.log1p(jnp.exp(-jnp.abs(z)))
    log_beta = z - sp
    if valid is not None:
        sp = jnp.where(valid, sp, 0.0)
    return sp, log_beta


def _hi_lo(x):
    hi = x.astype(BF16)
    return jnp.concatenate([hi, (x - hi.astype(F32)).astype(BF16)], axis=1)


def _sb_weight(log_beta, cum, car, valid):
    w = jnp.exp(log_beta - cum[:, :LANES] - car)
    if valid is not None:
        w = jnp.where(valid, w, 0.0)
    return w.astype(BF16), car + cum[:, LANES:]


def _sb_prompt_kernel(bias_ref, q_ref, k_ref, v_ref, w2_ref, o_ref, acc_ref, car_ref, bias_v_ref, *,
                      scale, group, row_splits, sweep_blocks):
    kvh = pl.program_id(1)
    i = pl.program_id(2)
    tq = q_ref.shape[0]
    rows = group * tq
    split = rows // row_splits
    q = q_ref[...]
    qs = jnp.concatenate([q[:, g * HD_A:(g + 1) * HD_A] for g in range(group)], axis=0).astype(BF16)
    w2 = w2_ref[...]
    for g in range(group):
        bias_v_ref[g * tq:(g + 1) * tq, :] = jnp.full((tq, LANES), bias_ref[kvh * group + g], F32)
    acc_ref[...] = jnp.zeros_like(acc_ref)
    car_ref[...] = jnp.zeros_like(car_ref)
    t_idx = lax.broadcasted_iota(jnp.int32, (split, tq), 0) % tq
    s_idx = lax.broadcasted_iota(jnp.int32, (split, tq), 1)
    rsl = lambda r: slice(r * split, (r + 1) * split)

    def sweep(start, n_blocks, diagonal):
        top_valid = s_idx < jnp.where(diagonal, t_idx, tq)
        valid_of = lambda b: top_valid if b == n_blocks - 1 else None
        st = start if isinstance(start, int) else pl.multiple_of(start, tq)
        kb = k_ref[pl.ds(st, n_blocks * tq), :].astype(BF16)
        vb = v_ref[pl.ds(st, n_blocks * tq), :].astype(BF16)
        order = list(reversed(range(n_blocks)))
        zs = [lax.dot_general(qs[rsl(r)], kb, _NT, preferred_element_type=F32) for r in range(row_splits)]
        log_betas, cums = [], []
        for r in range(row_splits):
            bias = bias_v_ref[rsl(r)]
            sps, lbs = [], {}
            for b in order:
                sp, lbs[b] = _sb_terms(zs[r][:, b * tq:(b + 1) * tq] * scale + bias, valid_of(b))
                sps.append(_hi_lo(sp))
            log_betas.append(lbs)
            cums.append(jnp.dot(jnp.concatenate(sps, axis=0), w2, preferred_element_type=F32))
        for r in range(row_splits):
            car = car_ref[rsl(r)]
            ws = {}
            for t, b in enumerate(order):
                ws[b], car = _sb_weight(log_betas[r][b], cums[r][t * split:(t + 1) * split], car, valid_of(b))
            car_ref[rsl(r)] = car
            w_all = jnp.concatenate([ws[b] for b in range(n_blocks)], axis=1)
            acc_ref[rsl(r)] += jnp.dot(w_all, vb, preferred_element_type=F32)

    n_full = (i + 1) // sweep_blocks

    def full(p, c):
        sweep((i + 1 - sweep_blocks * (p + 1)) * tq, sweep_blocks, p == 0)
        return c

    lax.fori_loop(0, n_full, full, 0)
    for rem in range(1, sweep_blocks):
        @pl.when((i + 1) % sweep_blocks == rem)
        def _():
            sweep(0, rem, n_full == 0)

    for g in range(group):
        o_ref[:, g * HD_A:(g + 1) * HD_A] = acc_ref[g * tq:(g + 1) * tq].astype(o_ref.dtype)


def sb_prompt(qkv, bias, *, n_seq, seq_len, name):
    m = qkv.shape[0]
    h_a = bias.shape[0]
    group = h_a // KVH_A
    tq = PAGE_SIZE
    nq = seq_len // tq
    qw = group * HD_A
    k_col0 = h_a
    v_col0 = k_col0 + KVH_A
    return pl.pallas_call(
        functools.partial(_sb_prompt_kernel, scale=HD_A ** -0.5, group=group, row_splits=2, sweep_blocks=4),
        grid=(n_seq, KVH_A, nq),
        in_specs=[
            pl.BlockSpec(memory_space=pltpu.SMEM),
            pl.BlockSpec((tq, qw), lambda b, h, i: (b * nq + i, h)),
            pl.BlockSpec((seq_len, HD_A), lambda b, h, i: (b, k_col0 + h)),
            pl.BlockSpec((seq_len, HD_A), lambda b, h, i: (b, v_col0 + h)),
            pl.BlockSpec((2 * LANES, 2 * LANES), lambda b, h, i: (0, 0)),
        ],
        out_specs=pl.BlockSpec((tq, qw), lambda b, h, i: (b * nq + i, h)),
        out_shape=jax.ShapeDtypeStruct((m, h_a * HD_A), BF16),
        scratch_shapes=[pltpu.VMEM((group * tq, HD_A), F32), pltpu.VMEM((group * tq, LANES), F32),
                        pltpu.VMEM((group * tq, LANES), F32)],
        compiler_params=_params(3),
        name=name,
    )(bias, qkv, qkv, qkv, _suffix_sum_matrix())


def _sb_decode_kernel(pt_ref, q_ref, bias_ref, w2_ref, *refs, scale, pps, group):
    k_refs = refs[:pps]
    v_refs = refs[pps:2 * pps]
    o_ref, acc_ref, car_ref = refs[2 * pps:]
    s = pl.program_id(1)

    @pl.when(s == 0)
    def _():
        acc_ref[...] = jnp.zeros_like(acc_ref)
        car_ref[...] = jnp.zeros_like(car_ref)

    q = q_ref[0].astype(BF16)
    n_chunks = k_refs[0].shape[0] // LANES
    row_kvh = lax.broadcasted_iota(jnp.int32, (q.shape[0], LANES), 0) // group
    col_kvh = lax.broadcasted_iota(jnp.int32, (q.shape[0], LANES), 1) % KVH_A
    own = row_kvh == col_kvh
    bias = bias_ref[...]
    terms = []
    for r in range(pps):
        kp = k_refs[r][...].astype(BF16)
        z = lax.dot_general(q, kp, _NT, preferred_element_type=F32) * scale
        for c in reversed(range(n_chunks)):
            terms.append(_sb_terms(z[:, c * LANES:(c + 1) * LANES] + bias, own))
    n_heads = q.shape[0]
    cum = jnp.dot(jnp.concatenate([_hi_lo(sp) for sp, _ in terms], axis=0), w2_ref[...], preferred_element_type=F32)
    car = car_ref[...]
    ws = []
    for t, (_, log_beta) in enumerate(terms):
        w, car = _sb_weight(log_beta, cum[t * n_heads:(t + 1) * n_heads], car, own)
        ws.append(w)
    car_ref[...] = car
    acc = acc_ref[...]
    for r in range(pps):
        w_page = jnp.concatenate([ws[r * n_chunks + (n_chunks - 1 - c)] for c in range(n_chunks)], axis=1)
        acc = acc + jnp.dot(w_page, v_refs[r][...].astype(BF16), preferred_element_type=F32)
    acc_ref[...] = acc

    @pl.when(s == pl.num_programs(1) - 1)
    def _():
        o_ref[0] = acc_ref[...]


def sb_decode(q, bias, cache_k, cache_v, page_table, *, layer, pps, name):
    n, h_a, _ = q.shape
    group = h_a // KVH_A
    n_pages = page_table.shape[1]
    page_rows = cache_k.shape[2]

    def page_spec(r):
        def index_map(b, s, pt):
            return (layer, pt[b * n_pages + (n_pages - 1 - (s * pps + r))], 0, 0)
        return pl.BlockSpec((None, None, page_rows, HD_A), index_map)

    grid_spec = pltpu.PrefetchScalarGridSpec(
        num_scalar_prefetch=1,
        grid=(n, n_pages // pps),
        in_specs=[
            pl.BlockSpec((1, h_a, HD_A), lambda b, s, pt: (b, 0, 0)),
            pl.BlockSpec((h_a, LANES), lambda b, s, pt: (0, 0)),
            pl.BlockSpec((2 * LANES, 2 * LANES), lambda b, s, pt: (0, 0)),
        ] + [page_spec(r) for r in range(pps)] * 2,
        out_specs=pl.BlockSpec((1, h_a, HD_A), lambda b, s, pt: (b, 0, 0)),
        scratch_shapes=[pltpu.VMEM((h_a, HD_A), F32), pltpu.VMEM((h_a, LANES), F32)],
    )
    bias_b = jnp.broadcast_to(bias.astype(F32)[:, None], (h_a, LANES))
    return pl.pallas_call(
        functools.partial(_sb_decode_kernel, scale=HD_A ** -0.5, pps=pps, group=group),
        grid_spec=grid_spec,
        out_shape=jax.ShapeDtypeStruct((n, h_a, HD_A), F32),
        compiler_params=_params(2),
        name=name,
    )(page_table.reshape(-1), q, bias_b, _suffix_sum_matrix(), *([cache_k] * pps), *([cache_v] * pps))


def _swa_prompt_kernel(sink_ref, q_ref, kc_ref, kp_ref, vc_ref, vp_ref, o_ref, *, scale, n_heads):
    i = pl.program_id(1)
    tq = q_ref.shape[0]
    half = HD_B
    lane = lax.broadcasted_iota(jnp.int32, (tq, LANES), 1)
    low = lane < half
    row = lax.broadcasted_iota(jnp.int32, (tq, tq), 0)
    col = lax.broadcasted_iota(jnp.int32, (tq, tq), 1)
    mask_cur = col <= row
    mask_prev = (col >= row) & (i > 0)
    mask = jnp.concatenate([mask_prev, mask_cur], axis=1)
    mask2 = jnp.concatenate([mask, mask], axis=0)
    low2 = jnp.concatenate([low, low], axis=0)
    group = n_heads // KVH_B
    for c in range(KVH_B // 2):
        kk = jnp.concatenate([kp_ref[:, c * LANES:(c + 1) * LANES], kc_ref[:, c * LANES:(c + 1) * LANES]], axis=0)
        vv = jnp.concatenate([vp_ref[:, c * LANES:(c + 1) * LANES], vc_ref[:, c * LANES:(c + 1) * LANES]], axis=0)
        kr = pltpu.roll(kk, half, 1)
        vr = pltpu.roll(vv, half, 1)
        for e in range(2):
            kvh = 2 * c + e
            kd = (jnp.where(low2, kk, kr) if e == 0 else jnp.where(low2, kr, kk)).astype(BF16)
            vd = (jnp.where(low2, vv, vr) if e == 0 else jnp.where(low2, vr, vv)).astype(BF16)
            for p in range(group // 2):
                h0 = kvh * group + 2 * p
                qcol = h0 // 2
                qp = q_ref[:, qcol * LANES:(qcol + 1) * LANES]
                q2 = jnp.concatenate([jnp.where(low, qp, 0.0), jnp.where(low, 0.0, qp)], axis=0).astype(BF16)
                s = lax.dot_general(q2, kd, _NT, preferred_element_type=F32) * scale
                s = jnp.where(mask2, s, -jnp.inf)
                rid = lax.broadcasted_iota(jnp.int32, (2 * tq, 1), 0)
                sink = jnp.where(rid < tq, sink_ref[h0], sink_ref[h0 + 1])
                m = jnp.maximum(jnp.max(s, axis=-1, keepdims=True), sink)
                pexp = jnp.exp(s - m)
                denom = jnp.sum(pexp, axis=-1, keepdims=True) + jnp.exp(sink - m)
                o2 = jnp.dot(pexp.astype(BF16), vd, preferred_element_type=F32) / denom
                o_ref[:, qcol * LANES:(qcol + 1) * LANES] = jnp.where(low, o2[:tq], o2[tq:]).astype(o_ref.dtype)


def swa_prompt(q, kv, sinks, *, n_seq, seq_len, name):
    m, qw = q.shape
    n_heads = sinks.shape[0]
    tq = WINDOW
    nq = seq_len // tq
    kw = KVH_B * HD_B
    cur = lambda col: (lambda b, i: (b * nq + i, col))
    prev = lambda col: (lambda b, i: (b * nq + jnp.maximum(i - 1, 0), col))
    return pl.pallas_call(
        functools.partial(_swa_prompt_kernel, scale=HD_B ** -0.5, n_heads=n_heads),
        grid=(n_seq, nq),
        in_specs=[
            pl.BlockSpec(memory_space=pltpu.SMEM),
            pl.BlockSpec((tq, qw), lambda b, i: (b * nq + i, 0)),
            pl.BlockSpec((tq, kw), cur(0)),
            pl.BlockSpec((tq, kw), prev(0)),
            pl.BlockSpec((tq, kw), cur(1)),
            pl.BlockSpec((tq, kw), prev(1)),
        ],
        out_specs=pl.BlockSpec((tq, qw), lambda b, i: (b * nq + i, 0)),
        out_shape=jax.ShapeDtypeStruct((m, qw), BF16),
        compiler_params=_params(2),
        name=name,
    )(sinks, q, kv, kv, kv, kv)


def _swa_decode_kernel(q_ref, sink_ref, kw_ref, vw_ref, kn_ref, vn_ref, o_ref, *, scale, group):
    q = q_ref[0]
    row_kvh = lax.broadcasted_iota(jnp.int32, q.shape, 0) // group
    lane_kvh = lax.broadcasted_iota(jnp.int32, q.shape, 1) // HD_B
    own = row_kvh == lane_kvh
    qbd = jnp.where(own, q, 0.0)
    qb = qbd.astype(BF16)
    kwin = kw_ref[0].astype(BF16)
    vwin = vw_ref[0].astype(BF16)
    kn = kn_ref[0]
    vn = vn_ref[0]
    s = lax.dot_general(qb, kwin, _NT, preferred_element_type=F32) * scale
    s_new = jnp.sum(qb.astype(F32) * kn.astype(BF16).astype(F32), axis=-1, keepdims=True) * scale
    sink = sink_ref[:, 0:1]
    m = jnp.maximum(jnp.maximum(jnp.max(s, axis=-1, keepdims=True), s_new), sink)
    p = jnp.exp(s - m)
    p_new = jnp.exp(s_new - m)
    denom = jnp.sum(p, axis=-1, keepdims=True) + p_new + jnp.exp(sink - m)
    o = jnp.dot(p.astype(BF16), vwin, preferred_element_type=F32)
    o = o + p_new.astype(BF16).astype(F32) * vn.astype(BF16).astype(F32)
    o = jnp.where(own, o / denom, 0.0)
    o = o + pltpu.roll(o, 2 * HD_B, 1)
    o = o + pltpu.roll(o, HD_B, 1)
    o_ref[0] = o[:, :HD_B]


def swa_decode(q, sinks, win_k, win_v, k_new, v_new, *, name):
    n, n_heads, _ = q.shape
    w = win_k.shape[1]
    kw = KVH_B * HD_B
    q_t = jnp.tile(q, (1, 1, KVH_B))
    sink_b = jnp.broadcast_to(sinks.astype(F32)[:, None], (n_heads, LANES))
    per_seq = lambda a, b: pl.BlockSpec((1, a, b), lambda s: (s, 0, 0))
    return pl.pallas_call(
        functools.partial(_swa_decode_kernel, scale=HD_B ** -0.5, group=n_heads // KVH_B),
        grid=(n,),
        in_specs=[
            per_seq(n_heads, kw),
            pl.BlockSpec((n_heads, LANES), lambda s: (0, 0)),
            per_seq(w, kw), per_seq(w, kw), per_seq(1, kw), per_seq(1, kw),
        ],
        out_specs=per_seq(n_heads, HD_B),
        out_shape=jax.ShapeDtypeStruct((n, n_heads, HD_B), F32),
        compiler_params=_params(1),
        name=name,
    )(q_t, sink_b, win_k, win_v, k_new, v_new)


def _rope_tables(pos):
    n = pos.shape[0]
    half = ROPE_DIM // 2
    inv_freq = 1.0 / (ROPE_THETA ** (jnp.arange(half, dtype=F32) * (2.0 / ROPE_DIM)))
    ang = pos.astype(F32)[:, None] * inv_freq[None, :]
    cos, sin = jnp.cos(ang), jnp.sin(ang)
    pad = lambda a, left, fill: jnp.concatenate(
        [jnp.full((n, left), fill, F32), a, jnp.full((n, HD_B - left - a.shape[1]), fill, F32)], axis=1)
    c = jnp.concatenate([cos, cos, jnp.ones((n, HD_B - ROPE_DIM), F32)], axis=1)
    sa = pad(-sin, 0, 0.0)
    sb = pad(sin, half, 0.0)
    two = lambda a: jnp.concatenate([a, a], axis=1)
    return two(c), two(sa), two(sb)


def kernel(x_prompt, x_sample, cache_a_k, cache_a_v, state_win_k, state_win_v, state_conv, page_table, norm_g, w_qkv_a, w_o_a, sb_bias, w_kv_b, kv_norm_g, w_q_b, w_o_b, sinks_b, w_up, conv_w, conv_b, w_down):
    n_seq, seq_len, d = x_prompt.shape
    n_dec = x_sample.shape[0]
    depth = norm_g.shape[0]
    n_a = w_qkv_a.shape[0]
    h_a = sb_bias.shape[1]
    h_b = sinks_b.shape[1]
    d_ff = w_down.shape[1]
    past_len = page_table.shape[1] * PAGE_SIZE
    qa_w = h_a * HD_A
    ka_w = KVH_A * HD_A
    kb_w = KVH_B * HD_B

    bf = lambda a: a.astype(BF16)
    w_qkv_a, w_o_a, w_kv_b, w_q_b, w_o_b, w_up, w_down = map(bf, (w_qkv_a, w_o_a, w_kv_b, w_q_b, w_o_b, w_up, w_down))

    rope_p = _rope_tables(jnp.arange(seq_len))
    rope_s = _rope_tables(jnp.full((n_dec,), past_len))
    cache_k = cache_a_k.reshape(cache_a_k.shape[:2] + (PAGE_SIZE * KVH_A, HD_A))
    cache_v = cache_a_v.reshape(cache_a_v.shape[:2] + (PAGE_SIZE * KVH_A, HD_A))
    win_k = state_win_k.reshape(n_dec, -1, kb_w)
    win_v = state_win_v.reshape(n_dec, -1, kb_w)

    x = x_prompt.reshape(n_seq * seq_len, d)
    ak_p, av_p, conv_p = [], [], []
    kv = None
    for l in range(depth):
        if l < n_a:
            qkv = norm_matmul(x, norm_g[l, 0], w_qkv_a[l], tm=512, tn=512, name=f"p{l}_qkv")
            ak_p.append(qkv[:, qa_w:qa_w + ka_w].reshape(n_seq, seq_len, KVH_A, HD_A))
            av_p.append(qkv[:, qa_w + ka_w:].reshape(n_seq, seq_len, KVH_A, HD_A))
            o = sb_prompt(qkv, sb_bias[l], n_seq=n_seq, seq_len=seq_len, name=f"p{l}_sb")
            x = matmul_norm_res(o, w_o_a[l], norm_g[l, 1], x, tm=256, name=f"p{l}_wo")
        else:
            jb = l - n_a
            if kv is None:
                kv = norm_matmul(x, kv_norm_g, w_kv_b, tm=512, tn=kb_w, rope=rope_p, n_rope_tiles=1, name="p_kv")
            qb = norm_matmul(x, norm_g[l, 0], w_q_b[jb], tm=512, tn=512, rope=rope_p,
                             n_rope_tiles=h_b * HD_B // 512, name=f"p{l}_q")
            o = swa_prompt(qb, kv, sinks_b[jb], n_seq=n_seq, seq_len=seq_len, name=f"p{l}_swa")
            x = matmul_norm_res(o, w_o_b[jb], norm_g[l, 1], x, tm=256, name=f"p{l}_wo")
        x, cog, cov = ffn(x, norm_g[l, 2], w_up[l], conv_w[l], conv_b[l], w_down[l], norm_g[l, 3],
                          seq_len=seq_len, tm=FFN_TM, tf=512, chunk=256, name=f"p{l}_ffn")
        last = slice(seq_len // FFN_TM - 1, None, seq_len // FFN_TM)
        conv_p.append(jnp.concatenate([cog[last, SUBLANES - 2:], cov[last, SUBLANES - 2:]], axis=-1))
    y_prompt = x.reshape(n_seq, seq_len, d)
    kv_p = kv.reshape(n_seq, seq_len, 2, KVH_B, HD_B)
    n_keep = min(WINDOW, seq_len)
    wk_p = kv_p[:, seq_len - n_keep:, 0]
    wv_p = kv_p[:, seq_len - n_keep:, 1]

    x = x_sample.reshape(n_dec, d)
    ak_s, av_s, conv_s = [], [], []
    kv = None
    for l in range(depth):
        if l < n_a:
            qkv = norm_matmul(x, norm_g[l, 0], w_qkv_a[l], tm=n_dec, tn=512, name=f"s{l}_qkv")
            ak_s.append(qkv[:, qa_w:qa_w + ka_w].reshape(n_dec, 1, KVH_A, HD_A))
            av_s.append(qkv[:, qa_w + ka_w:].reshape(n_dec, 1, KVH_A, HD_A))
            o = sb_decode(qkv[:, :qa_w].reshape(n_dec, h_a, HD_A), sb_bias[l], cache_k, cache_v, page_table,
                          layer=l, pps=8, name=f"s{l}_sb")
            x = matmul_norm_res(o.reshape(n_dec, qa_w), w_o_a[l], norm_g[l, 1], x, tm=n_dec, name=f"s{l}_wo")
        else:
            jb = l - n_a
            if kv is None:
                kv = norm_matmul(x, kv_norm_g, w_kv_b, tm=n_dec, tn=kb_w, rope=rope_s, n_rope_tiles=1, name="s_kv")
                k_new = kv[:, None, :kb_w]
                v_new = kv[:, None, kb_w:]
            qb = norm_matmul(x, norm_g[l, 0], w_q_b[jb], tm=n_dec, tn=512, rope=rope_s,
                             n_rope_tiles=h_b * HD_B // 512, name=f"s{l}_q")
            o = swa_decode(qb.reshape(n_dec, h_b, HD_B), sinks_b[jb], win_k, win_v, k_new, v_new, name=f"s{l}_swa")
            x = matmul_norm_res(o.reshape(n_dec, h_b * HD_B), w_o_b[jb], norm_g[l, 1], x, tm=n_dec, name=f"s{l}_wo")
        x, u_new = ffn_step(x, norm_g[l, 2], w_up[l], conv_w[l], conv_b[l],
                            state_conv[l, :, 0], state_conv[l, :, 1], w_down[l], norm_g[l, 3],
                            tf=512, name=f"s{l}_ffn")
        conv_s.append(jnp.concatenate([state_conv[l, :, 1:], u_new[:, None, :]], axis=1)[:, -(CONV_W - 1):])
    y_sample = x.reshape(n_dec, 1, d)
    n_keep_s = min(WINDOW, win_k.shape[1] + 1)
    wk_s = jnp.concatenate([win_k, k_new], axis=1)[:, -n_keep_s:].reshape(n_dec, n_keep_s, KVH_B, HD_B)
    wv_s = jnp.concatenate([win_v, v_new], axis=1)[:, -n_keep_s:].reshape(n_dec, n_keep_s, KVH_B, HD_B)

    return (y_prompt, y_sample, jnp.stack(ak_p), jnp.stack(av_p), jnp.stack(ak_s), jnp.stack(av_s),
            wk_p, wv_p, wk_s, wv_s, jnp.stack(conv_p), jnp.stack(conv_s))
```

```python
import functools

import jax
import jax.numpy as jnp
from jax import lax
from jax.experimental import pallas as pl
from jax.experimental.pallas import tpu as pltpu

F32 = jnp.float32
BF16 = jnp.bfloat16

EPS = 1e-6
PAGE_SIZE = 128
HD_A = 128
KVH_A = 4
HD_B = 64
KVH_B = 4
WINDOW = 128
ROPE_DIM = HD_B // 4
ROPE_THETA = 500000.0
CONV_W = 3

LANES = 128
SUBLANES = 8
VMEM_LIMIT = 56 * 1024 * 1024
FFN_TM = 512

_NT = (((1,), (1,)), ((), ()))


def _params(n_axes, vmem=VMEM_LIMIT):
    return pltpu.CompilerParams(dimension_semantics=("arbitrary",) * n_axes, vmem_limit_bytes=vmem)


def _rms(x, g):
    ms = jnp.mean(x * x, axis=-1, keepdims=True)
    return x * lax.rsqrt(ms + EPS) * g


COL_CHUNK = 512


def _norm_matmul_kernel(*refs, rope_cols, kv_col0, kv_heads):
    refs = list(refs)
    x_ref, g_ref, w_ref = refs[:3]
    rope_refs = refs[3:6] if rope_cols else ()
    o_ref = refs[3 + len(rope_refs)]
    kv_refs = refs[4 + len(rope_refs):]
    tm, n = o_ref.shape
    h = _rms(x_ref[...], g_ref[...]).astype(BF16)
    if rope_cols:
        c, sa, sb = (r[...] for r in rope_refs)
    for c0 in range(0, n, COL_CHUNK):
        acc = jnp.dot(h, w_ref[:, c0:c0 + COL_CHUNK], preferred_element_type=F32)
        for t in range(0, min(COL_CHUNK, n - c0), LANES):
            a = acc[:, t:t + LANES]
            col = c0 + t
            if col < rope_cols:
                a = a * c + pltpu.roll(a, LANES - ROPE_DIM // 2, 1) * sa + pltpu.roll(a, ROPE_DIM // 2, 1) * sb
            o_ref[:, col:col + LANES] = a
            if kv_refs and col >= kv_col0:
                which, head = divmod((col - kv_col0) // LANES, kv_heads)
                kv_refs[which][pl.ds(head, tm, stride=kv_heads), :] = a


def norm_matmul(x, g, w, layer, *, tm, rope=None, rope_cols=0, kv_col0=None, kv_heads=0, name):
    m, d = x.shape
    n = w.shape[2]
    in_specs = [
        pl.BlockSpec((tm, d), lambda i: (i, 0)),
        pl.BlockSpec((1, d), lambda i: (0, 0)),
        pl.BlockSpec((None, d, n), lambda i: (layer, 0, 0)),
    ]
    args = [x, g.reshape(1, d), w]
    if rope_cols:
        n_tab = rope[0].shape[0] // tm
        for t in rope:
            in_specs.append(pl.BlockSpec((tm, LANES), lambda i: (i % n_tab, 0)))
            args.append(t)
    out_specs = [pl.BlockSpec((tm, n), lambda i: (i, 0))]
    out_shape = [jax.ShapeDtypeStruct((m, n), F32)]
    if kv_col0 is not None:
        out_specs += [pl.BlockSpec((tm * kv_heads, LANES), lambda i: (i, 0))] * 2
        out_shape += [jax.ShapeDtypeStruct((m * kv_heads, LANES), F32)] * 2
    out = pl.pallas_call(
        functools.partial(_norm_matmul_kernel, rope_cols=rope_cols, kv_col0=kv_col0, kv_heads=kv_heads),
        grid=(m // tm,),
        in_specs=in_specs,
        out_specs=out_specs,
        out_shape=out_shape,
        compiler_params=_params(1),
        name=name,
    )(*args)
    return out if kv_col0 is not None else out[0]


def _matmul_norm_res_kernel(a_ref, w_ref, g_ref, x_ref, o_ref):
    m = jnp.dot(a_ref[...].astype(BF16), w_ref[...], preferred_element_type=F32)
    o_ref[...] = x_ref[...] + _rms(m, g_ref[...])


def matmul_norm_res(a, w, layer, g, x, *, tm, name):
    m, k = a.shape
    n = w.shape[2]
    return pl.pallas_call(
        _matmul_norm_res_kernel,
        grid=(m // tm,),
        in_specs=[
            pl.BlockSpec((tm, k), lambda i: (i, 0)),
            pl.BlockSpec((None, k, n), lambda i: (layer, 0, 0)),
            pl.BlockSpec((1, n), lambda i: (0, 0)),
            pl.BlockSpec((tm, n), lambda i: (i, 0)),
        ],
        out_specs=pl.BlockSpec((tm, n), lambda i: (i, 0)),
        out_shape=jax.ShapeDtypeStruct((m, n), F32),
        compiler_params=_params(1),
        name=name,
    )(a, w, g.reshape(1, n), x)


def _ffn_kernel(x_ref, g2_ref, wg_ref, wv_ref, cwg_ref, cwv_ref, cbg_ref, cbv_ref, wd_ref, g3_ref,
                o_ref, cog_ref, cov_ref,
                h_ref, acc_ref, carg_ref, carv_ref, *buf_refs, tiles_per_seq):
    i = pl.program_id(0)
    j = pl.program_id(1)
    tm = x_ref.shape[0]
    n_chunks = len(buf_refs) // 2
    bufg_refs, bufv_refs = buf_refs[:n_chunks], buf_refs[n_chunks:]
    chunk = bufg_refs[0].shape[1]
    col = lambda c: slice(c * chunk, (c + 1) * chunk)

    @pl.when(j == 0)
    def _():
        h_ref[...] = _rms(x_ref[...], g2_ref[...]).astype(BF16)
        acc_ref[...] = jnp.zeros_like(acc_ref)

    seq_start = (i % tiles_per_seq) == 0

    @pl.when(seq_start)
    def _():
        for buf_ref in buf_refs:
            buf_ref[0:SUBLANES, :] = jnp.zeros((SUBLANES, chunk), F32)

    @pl.when(jnp.logical_not(seq_start))
    def _():
        for c in range(n_chunks):
            bufg_refs[c][0:SUBLANES, :] = carg_ref[j, :, col(c)]
            bufv_refs[c][0:SUBLANES, :] = carv_ref[j, :, col(c)]

    for c in range(n_chunks):
        bufg_refs[c][SUBLANES:SUBLANES + tm, :] = jnp.dot(h_ref[...], wg_ref[:, col(c)], preferred_element_type=F32)
        bufv_refs[c][SUBLANES:SUBLANES + tm, :] = jnp.dot(h_ref[...], wv_ref[:, col(c)], preferred_element_type=F32)

    def conv(c, cw_ref, cb_ref, buf_ref, car_ref, co_ref):
        tail = buf_ref[tm:tm + SUBLANES, :]
        car_ref[j, :, col(c)] = tail
        co_ref[0, :, col(c)] = tail
        cw = cw_ref[:, col(c)]
        y = cb_ref[:, col(c)] + cw[0:1] * buf_ref[SUBLANES - 2:SUBLANES - 2 + tm, :]
        y = y + cw[1:2] * buf_ref[SUBLANES - 1:SUBLANES - 1 + tm, :]
        return y + cw[2:3] * buf_ref[SUBLANES:SUBLANES + tm, :]

    for c in range(n_chunks):
        cg = conv(c, cwg_ref, cbg_ref, bufg_refs[c], carg_ref, cog_ref)
        cv = conv(c, cwv_ref, cbv_ref, bufv_refs[c], carv_ref, cov_ref)
        act = (jax.nn.gelu(cg, approximate=True) * cv).astype(BF16)
        acc_ref[...] += jnp.dot(act, wd_ref[col(c), :], preferred_element_type=F32)

    @pl.when(j == pl.num_programs(1) - 1)
    def _():
        o_ref[...] = x_ref[...] + _rms(acc_ref[...], g3_ref[...])


def ffn(x, g2, w_up, conv_w, conv_b, w_down, g3, layer, *, seq_len, tm, tf, chunk, name):
    m, d = x.shape
    d_ff = w_down.shape[1]
    nj = d_ff // tf
    tiles_per_seq = seq_len // tm
    conv_b2 = conv_b.reshape(1, 2 * d_ff)
    out, cog, cov = pl.pallas_call(
        functools.partial(_ffn_kernel, tiles_per_seq=tiles_per_seq),
        grid=(m // tm, nj),
        in_specs=[
            pl.BlockSpec((tm, d), lambda i, j: (i, 0)),
            pl.BlockSpec((1, d), lambda i, j: (0, 0)),
            pl.BlockSpec((None, d, tf), lambda i, j: (layer, 0, j)),
            pl.BlockSpec((None, d, tf), lambda i, j: (layer, 0, j + nj)),
            pl.BlockSpec((CONV_W, tf), lambda i, j: (0, j)),
            pl.BlockSpec((CONV_W, tf), lambda i, j: (0, j + nj)),
            pl.BlockSpec((1, tf), lambda i, j: (0, j)),
            pl.BlockSpec((1, tf), lambda i, j: (0, j + nj)),
            pl.BlockSpec((None, tf, d), lambda i, j: (layer, j, 0)),
            pl.BlockSpec((1, d), lambda i, j: (0, 0)),
        ],
        out_specs=[
            pl.BlockSpec((tm, d), lambda i, j: (i, 0)),
            pl.BlockSpec((1, SUBLANES, tf), lambda i, j: (i, 0, j)),
            pl.BlockSpec((1, SUBLANES, tf), lambda i, j: (i, 0, j)),
        ],
        out_shape=[
            jax.ShapeDtypeStruct((m, d), F32),
            jax.ShapeDtypeStruct((m // tm, SUBLANES, d_ff), F32),
            jax.ShapeDtypeStruct((m // tm, SUBLANES, d_ff), F32),
        ],
        scratch_shapes=[
            pltpu.VMEM((tm, d), BF16),
            pltpu.VMEM((tm, d), F32),
            pltpu.VMEM((nj, SUBLANES, tf), F32),
            pltpu.VMEM((nj, SUBLANES, tf), F32),
        ] + [pltpu.VMEM((tm + SUBLANES, chunk), F32)] * (2 * (tf // chunk)),
        compiler_params=_params(2),
        name=name,
    )(x, g2.reshape(1, d), w_up, w_up, conv_w, conv_w, conv_b2, conv_b2, w_down, g3.reshape(1, d))
    return out, cog, cov


def _ffn_step_kernel(x_ref, g2_ref, wg_ref, wv_ref, cwg_ref, cwv_ref, cbg_ref, cbv_ref,
                     p0g_ref, p0v_ref, p1g_ref, p1v_ref, wd_ref, g3_ref,
                     o_ref, ug_ref, uv_ref, h_ref, acc_ref):
    j = pl.program_id(0)

    @pl.when(j == 0)
    def _():
        h_ref[...] = _rms(x_ref[...], g2_ref[...]).astype(BF16)
        acc_ref[...] = jnp.zeros_like(acc_ref)

    h = h_ref[...]

    def conv(w_ref, cw_ref, cb_ref, p0_ref, p1_ref, u_ref):
        u = jnp.dot(h, w_ref[...], preferred_element_type=F32)
        u_ref[...] = u
        cw = cw_ref[...]
        c = cb_ref[...] + cw[0:1] * p0_ref[...]
        c = c + cw[1:2] * p1_ref[...]
        return c + cw[2:3] * u

    cg = conv(wg_ref, cwg_ref, cbg_ref, p0g_ref, p1g_ref, ug_ref)
    cv = conv(wv_ref, cwv_ref, cbv_ref, p0v_ref, p1v_ref, uv_ref)
    act = (jax.nn.gelu(cg, approximate=True) * cv).astype(BF16)
    acc_ref[...] += jnp.dot(act, wd_ref[...], preferred_element_type=F32)

    @pl.when(j == pl.num_programs(0) - 1)
    def _():
        o_ref[...] = x_ref[...] + _rms(acc_ref[...], g3_ref[...])


def ffn_step(x, g2, w_up, conv_w, conv_b, prev0, prev1, w_down, g3, layer, *, tf, name):
    m, d = x.shape
    d_ff = w_down.shape[1]
    nj = d_ff // tf
    conv_b2 = conv_b.reshape(1, 2 * d_ff)
    lo = lambda j: (0, j)
    hi = lambda j: (0, j + nj)
    out, ug, uv = pl.pallas_call(
        _ffn_step_kernel,
        grid=(nj,),
        in_specs=[
            pl.BlockSpec((m, d), lambda j: (0, 0)),
            pl.BlockSpec((1, d), lambda j: (0, 0)),
            pl.BlockSpec((None, d, tf), lambda j: (layer, 0, j)),
            pl.BlockSpec((None, d, tf), lambda j: (layer, 0, j + nj)),
            pl.BlockSpec((CONV_W, tf), lo),
            pl.BlockSpec((CONV_W, tf), hi),
            pl.BlockSpec((1, tf), lo),
            pl.BlockSpec((1, tf), hi),
            pl.BlockSpec((m, tf), lo),
            pl.BlockSpec((m, tf), hi),
            pl.BlockSpec((m, tf), lo),
            pl.BlockSpec((m, tf), hi),
            pl.BlockSpec((None, tf, d), lambda j: (layer, j, 0)),
            pl.BlockSpec((1, d), lambda j: (0, 0)),
        ],
        out_specs=[
            pl.BlockSpec((m, d), lambda j: (0, 0)),
            pl.BlockSpec((m, tf), lo),
            pl.BlockSpec((m, tf), lo),
        ],
        out_shape=[
            jax.ShapeDtypeStruct((m, d), F32),
            jax.ShapeDtypeStruct((m, d_ff), F32),
            jax.ShapeDtypeStruct((m, d_ff), F32),
        ],
        scratch_shapes=[pltpu.VMEM((m, d), BF16), pltpu.VMEM((m, d), F32)],
        compiler_params=_params(1),
        name=name,
    )(x, g2.reshape(1, d), w_up, w_up, conv_w, conv_w, conv_b2, conv_b2,
      prev0, prev0, prev1, prev1, w_down, g3.reshape(1, d))
    return out, jnp.concatenate([ug, uv], axis=-1)


def _suffix_sum_matrix():
    j = lax.broadcasted_iota(jnp.int32, (2 * LANES, 2 * LANES), 0) % LANES
    s = lax.broadcasted_iota(jnp.int32, (2 * LANES, 2 * LANES), 1)
    return jnp.where((s >= LANES) | (j > s), 1.0, 0.0).astype(BF16)


def _sb_terms(z, valid):
    sp = jnp.maximum(z, 0.0) + jnp.log(1.0 + jnp.exp(-jnp.abs(z)))
    log_beta = z - sp
    if valid is not None:
        sp = jnp.where(valid, sp, 0.0)
    return sp, log_beta


def _hi_lo(x):
    hi = x.astype(BF16)
    return jnp.concatenate([hi, (x - hi.astype(F32)).astype(BF16)], axis=1)


def _sb_weight(log_beta, cum, car, valid):
    w = jnp.exp(log_beta - cum[:, :LANES] - car)
    if valid is not None:
        w = jnp.where(valid, w, 0.0)
    return w.astype(BF16), car + cum[:, LANES:]


def _sb_prompt_kernel(bias_ref, q_ref, k_ref, v_ref, w2_ref, o_ref, acc_ref, car_ref, bias_v_ref, *,
                      scale, group, row_splits, sweep_blocks):
    kvh = pl.program_id(1)
    i = pl.program_id(2)
    tq = q_ref.shape[0]
    rows = group * tq
    split = rows // row_splits
    q = q_ref[...]
    qs = jnp.concatenate([q[:, g * HD_A:(g + 1) * HD_A] for g in range(group)], axis=0).astype(BF16)
    w2 = w2_ref[...]
    for g in range(group):
        bias_v_ref[g * tq:(g + 1) * tq, :] = jnp.full((tq, LANES), bias_ref[kvh * group + g], F32)
    acc_ref[...] = jnp.zeros_like(acc_ref)
    car_ref[...] = jnp.zeros_like(car_ref)
    t_idx = lax.broadcasted_iota(jnp.int32, (split, tq), 0) % tq
    s_idx = lax.broadcasted_iota(jnp.int32, (split, tq), 1)
    rsl = lambda r: slice(r * split, (r + 1) * split)

    def sweep(start, n_blocks, diagonal):
        top_valid = s_idx < jnp.where(diagonal, t_idx, tq)
        valid_of = lambda b: top_valid if b == n_blocks - 1 else None
        st = start if isinstance(start, int) else pl.multiple_of(start, tq)
        kb = k_ref[pl.ds(st, n_blocks * tq), :].astype(BF16)
        vb = v_ref[pl.ds(st, n_blocks * tq), :].astype(BF16)
        order = list(reversed(range(n_blocks)))
        zs = [lax.dot_general(qs[rsl(r)], kb, _NT, preferred_element_type=F32) for r in range(row_splits)]
        log_betas, cums = [], []
        for r in range(row_splits):
            bias = bias_v_ref[rsl(r)]
            sps, lbs = [], {}
            for b in order:
                sp, lbs[b] = _sb_terms(zs[r][:, b * tq:(b + 1) * tq] * scale + bias, valid_of(b))
                sps.append(_hi_lo(sp))
            log_betas.append(lbs)
            cums.append(jnp.dot(jnp.concatenate(sps, axis=0), w2, preferred_element_type=F32))
        for r in range(row_splits):
            car = car_ref[rsl(r)]
            ws = {}
            for t, b in enumerate(order):
                ws[b], car = _sb_weight(log_betas[r][b], cums[r][t * split:(t + 1) * split], car, valid_of(b))
            car_ref[rsl(r)] = car
            w_all = jnp.concatenate([ws[b] for b in range(n_blocks)], axis=1)
            acc_ref[rsl(r)] += jnp.dot(w_all, vb, preferred_element_type=F32)

    n_full = (i + 1) // sweep_blocks

    def full(p, c):
        sweep((i + 1 - sweep_blocks * (p + 1)) * tq, sweep_blocks, p == 0)
        return c

    lax.fori_loop(0, n_full, full, 0)
    for rem in range(1, sweep_blocks):
        @pl.when((i + 1) % sweep_blocks == rem)
        def _():
            sweep(0, rem, n_full == 0)

    for g in range(group):
        o_ref[:, g * HD_A:(g + 1) * HD_A] = acc_ref[g * tq:(g + 1) * tq].astype(o_ref.dtype)


def sb_prompt(qkv, bias, *, n_seq, seq_len, name):
    m = qkv.shape[0]
    h_a = bias.shape[0]
    group = h_a // KVH_A
    tq = PAGE_SIZE
    nq = seq_len // tq
    qw = group * HD_A
    k_col0 = h_a
    v_col0 = k_col0 + KVH_A
    return pl.pallas_call(
        functools.partial(_sb_prompt_kernel, scale=HD_A ** -0.5, group=group, row_splits=2, sweep_blocks=4),
        grid=(n_seq, KVH_A, nq),
        in_specs=[
            pl.BlockSpec(memory_space=pltpu.SMEM),
            pl.BlockSpec((tq, qw), lambda b, h, i: (b * nq + i, h)),
            pl.BlockSpec((seq_len, HD_A), lambda b, h, i: (b, k_col0 + h)),
            pl.BlockSpec((seq_len, HD_A), lambda b, h, i: (b, v_col0 + h)),
            pl.BlockSpec((2 * LANES, 2 * LANES), lambda b, h, i: (0, 0)),
        ],
        out_specs=pl.BlockSpec((tq, qw), lambda b, h, i: (b * nq + i, h)),
        out_shape=jax.ShapeDtypeStruct((m, h_a * HD_A), BF16),
        scratch_shapes=[pltpu.VMEM((group * tq, HD_A), F32), pltpu.VMEM((group * tq, LANES), F32),
                        pltpu.VMEM((group * tq, LANES), F32)],
        compiler_params=_params(3),
        name=name,
    )(bias, qkv, qkv, qkv, _suffix_sum_matrix())


def _sb_decode_kernel(pt_ref, q_ref, bias_ref, w2_ref, *refs, scale, pps, group):
    k_refs = refs[:pps]
    v_refs = refs[pps:2 * pps]
    o_ref, acc_ref, car_ref = refs[2 * pps:]
    s = pl.program_id(1)

    @pl.when(s == 0)
    def _():
        acc_ref[...] = jnp.zeros_like(acc_ref)
        car_ref[...] = jnp.zeros_like(car_ref)

    q = q_ref[0].astype(BF16)
    n_chunks = k_refs[0].shape[0] // LANES
    row_kvh = lax.broadcasted_iota(jnp.int32, (q.shape[0], LANES), 0) // group
    col_kvh = lax.broadcasted_iota(jnp.int32, (q.shape[0], LANES), 1) % KVH_A
    own = row_kvh == col_kvh
    bias = bias_ref[...]
    terms = []
    for r in range(pps):
        kp = k_refs[r][...].astype(BF16)
        z = lax.dot_general(q, kp, _NT, preferred_element_type=F32) * scale
        for c in reversed(range(n_chunks)):
            terms.append(_sb_terms(z[:, c * LANES:(c + 1) * LANES] + bias, own))
    n_heads = q.shape[0]
    cum = jnp.dot(jnp.concatenate([_hi_lo(sp) for sp, _ in terms], axis=0), w2_ref[...], preferred_element_type=F32)
    car = car_ref[...]
    ws = []
    for t, (_, log_beta) in enumerate(terms):
        w, car = _sb_weight(log_beta, cum[t * n_heads:(t + 1) * n_heads], car, own)
        ws.append(w)
    car_ref[...] = car
    acc = acc_ref[...]
    for r in range(pps):
        w_page = jnp.concatenate([ws[r * n_chunks + (n_chunks - 1 - c)] for c in range(n_chunks)], axis=1)
        acc = acc + jnp.dot(w_page, v_refs[r][...].astype(BF16), preferred_element_type=F32)
    acc_ref[...] = acc

    @pl.when(s == pl.num_programs(1) - 1)
    def _():
        o_ref[0] = acc_ref[...]


def sb_decode(q, bias, cache_k, cache_v, page_table, *, layer, pps, name):
    n, h_a, _ = q.shape
    group = h_a // KVH_A
    n_pages = page_table.shape[1]
    page_rows = cache_k.shape[2]

    def page_spec(r):
        def index_map(b, s, pt):
            return (layer, pt[b * n_pages + (n_pages - 1 - (s * pps + r))], 0, 0)
        return pl.BlockSpec((None, None, page_rows, HD_A), index_map)

    grid_spec = pltpu.PrefetchScalarGridSpec(
        num_scalar_prefetch=1,
        grid=(n, n_pages // pps),
        in_specs=[
            pl.BlockSpec((1, h_a, HD_A), lambda b, s, pt: (b, 0, 0)),
            pl.BlockSpec((h_a, LANES), lambda b, s, pt: (0, 0)),
            pl.BlockSpec((2 * LANES, 2 * LANES), lambda b, s, pt: (0, 0)),
        ] + [page_spec(r) for r in range(pps)] * 2,
        out_specs=pl.BlockSpec((1, h_a, HD_A), lambda b, s, pt: (b, 0, 0)),
        scratch_shapes=[pltpu.VMEM((h_a, HD_A), F32), pltpu.VMEM((h_a, LANES), F32)],
    )
    bias_b = jnp.broadcast_to(bias.astype(F32)[:, None], (h_a, LANES))
    return pl.pallas_call(
        functools.partial(_sb_decode_kernel, scale=HD_A ** -0.5, pps=pps, group=group),
        grid_spec=grid_spec,
        out_shape=jax.ShapeDtypeStruct((n, h_a, HD_A), F32),
        compiler_params=_params(2),
        name=name,
    )(page_table.reshape(-1), q, bias_b, _suffix_sum_matrix(), *([cache_k] * pps), *([cache_v] * pps))


def _swa_prompt_kernel(sink_ref, q_ref, kc_ref, kp_ref, vc_ref, vp_ref, o_ref, *, scale, n_heads):
    i = pl.program_id(1)
    tq = q_ref.shape[0]
    half = HD_B
    lane = lax.broadcasted_iota(jnp.int32, (tq, LANES), 1)
    low = lane < half
    row = lax.broadcasted_iota(jnp.int32, (tq, tq), 0)
    col = lax.broadcasted_iota(jnp.int32, (tq, tq), 1)
    mask_cur = col <= row
    mask_prev = (col >= row) & (i > 0)
    mask = jnp.concatenate([mask_prev, mask_cur], axis=1)
    mask2 = jnp.concatenate([mask, mask], axis=0)
    low2 = jnp.concatenate([low, low], axis=0)
    group = n_heads // KVH_B
    for c in range(KVH_B // 2):
        kk = jnp.concatenate([kp_ref[:, c * LANES:(c + 1) * LANES], kc_ref[:, c * LANES:(c + 1) * LANES]], axis=0)
        vv = jnp.concatenate([vp_ref[:, c * LANES:(c + 1) * LANES], vc_ref[:, c * LANES:(c + 1) * LANES]], axis=0)
        kr = pltpu.roll(kk, half, 1)
        vr = pltpu.roll(vv, half, 1)
        for e in range(2):
            kvh = 2 * c + e
            kd = (jnp.where(low2, kk, kr) if e == 0 else jnp.where(low2, kr, kk)).astype(BF16)
            vd = (jnp.where(low2, vv, vr) if e == 0 else jnp.where(low2, vr, vv)).astype(BF16)
            for p in range(group // 2):
                h0 = kvh * group + 2 * p
                qcol = h0 // 2
                qp = q_ref[:, qcol * LANES:(qcol + 1) * LANES]
                q2 = jnp.concatenate([jnp.where(low, qp, 0.0), jnp.where(low, 0.0, qp)], axis=0).astype(BF16)
                s = lax.dot_general(q2, kd, _NT, preferred_element_type=F32) * scale
                s = jnp.where(mask2, s, -jnp.inf)
                rid = lax.broadcasted_iota(jnp.int32, (2 * tq, 1), 0)
                sink = jnp.where(rid < tq, sink_ref[h0], sink_ref[h0 + 1])
                m = jnp.maximum(jnp.max(s, axis=-1, keepdims=True), sink)
                pexp = jnp.exp(s - m)
                denom = jnp.sum(pexp, axis=-1, keepdims=True) + jnp.exp(sink - m)
                o2 = jnp.dot(pexp.astype(BF16), vd, preferred_element_type=F32) / denom
                o_ref[:, qcol * LANES:(qcol + 1) * LANES] = jnp.where(low, o2[:tq], o2[tq:]).astype(o_ref.dtype)


def swa_prompt(q, kv, sinks, *, n_seq, seq_len, name):
    m, qw = q.shape
    n_heads = sinks.shape[0]
    tq = WINDOW
    nq = seq_len // tq
    kw = KVH_B * HD_B
    cur = lambda col: (lambda b, i: (b * nq + i, col))
    prev = lambda col: (lambda b, i: (b * nq + jnp.maximum(i - 1, 0), col))
    return pl.pallas_call(
        functools.partial(_swa_prompt_kernel, scale=HD_B ** -0.5, n_heads=n_heads),
        grid=(n_seq, nq),
        in_specs=[
            pl.BlockSpec(memory_space=pltpu.SMEM),
            pl.BlockSpec((tq, qw), lambda b, i: (b * nq + i, 0)),
            pl.BlockSpec((tq, kw), cur(0)),
            pl.BlockSpec((tq, kw), prev(0)),
            pl.BlockSpec((tq, kw), cur(1)),
            pl.BlockSpec((tq, kw), prev(1)),
        ],
        out_specs=pl.BlockSpec((tq, qw), lambda b, i: (b * nq + i, 0)),
        out_shape=jax.ShapeDtypeStruct((m, qw), BF16),
        compiler_params=_params(2),
        name=name,
    )(sinks, q, kv, kv, kv, kv)


def _swa_decode_kernel(q_ref, sink_ref, kw_ref, vw_ref, kn_ref, vn_ref, o_ref, *, scale, group):
    q = q_ref[0]
    row_kvh = lax.broadcasted_iota(jnp.int32, q.shape, 0) // group
    lane_kvh = lax.broadcasted_iota(jnp.int32, q.shape, 1) // HD_B
    own = row_kvh == lane_kvh
    qbd = jnp.where(own, q, 0.0)
    qb = qbd.astype(BF16)
    kwin = kw_ref[0].astype(BF16)
    vwin = vw_ref[0].astype(BF16)
    kn = kn_ref[0]
    vn = vn_ref[0]
    s = lax.dot_general(qb, kwin, _NT, preferred_element_type=F32) * scale
    s_new = jnp.sum(qb.astype(F32) * kn.astype(BF16).astype(F32), axis=-1, keepdims=True) * scale
    sink = sink_ref[:, 0:1]
    m = jnp.maximum(jnp.maximum(jnp.max(s, axis=-1, keepdims=True), s_new), sink)
    p = jnp.exp(s - m)
    p_new = jnp.exp(s_new - m)
    denom = jnp.sum(p, axis=-1, keepdims=True) + p_new + jnp.exp(sink - m)
    o = jnp.dot(p.astype(BF16), vwin, preferred_element_type=F32)
    o = o + p_new.astype(BF16).astype(F32) * vn.astype(BF16).astype(F32)
    o = jnp.where(own, o / denom, 0.0)
    o = o + pltpu.roll(o, 2 * HD_B, 1)
    o = o + pltpu.roll(o, HD_B, 1)
    o_ref[0] = o[:, :HD_B]


def swa_decode(q, sinks, win_k, win_v, k_new, v_new, *, name):
    n, n_heads, _ = q.shape
    w = win_k.shape[1]
    kw = KVH_B * HD_B
    q_t = jnp.tile(q, (1, 1, KVH_B))
    sink_b = jnp.broadcast_to(sinks.astype(F32)[:, None], (n_heads, LANES))
    per_seq = lambda a, b: pl.BlockSpec((1, a, b), lambda s: (s, 0, 0))
    return pl.pallas_call(
        functools.partial(_swa_decode_kernel, scale=HD_B ** -0.5, group=n_heads // KVH_B),
        grid=(n,),
        in_specs=[
            per_seq(n_heads, kw),
            pl.BlockSpec((n_heads, LANES), lambda s: (0, 0)),
            per_seq(w, kw), per_seq(w, kw), per_seq(1, kw), per_seq(1, kw),
        ],
        out_specs=per_seq(n_heads, HD_B),
        out_shape=jax.ShapeDtypeStruct((n, n_heads, HD_B), F32),
        compiler_params=_params(1),
        name=name,
    )(q_t, sink_b, win_k, win_v, k_new, v_new)


def _rope_tables(pos):
    n = pos.shape[0]
    half = ROPE_DIM // 2
    inv_freq = 1.0 / (ROPE_THETA ** (jnp.arange(half, dtype=F32) * (2.0 / ROPE_DIM)))
    ang = pos.astype(F32)[:, None] * inv_freq[None, :]
    cos, sin = jnp.cos(ang), jnp.sin(ang)
    pad = lambda a, left, fill: jnp.concatenate(
        [jnp.full((n, left), fill, F32), a, jnp.full((n, HD_B - left - a.shape[1]), fill, F32)], axis=1)
    c = jnp.concatenate([cos, cos, jnp.ones((n, HD_B - ROPE_DIM), F32)], axis=1)
    sa = pad(-sin, 0, 0.0)
    sb = pad(sin, half, 0.0)
    two = lambda a: jnp.concatenate([a, a], axis=1)
    return two(c), two(sa), two(sb)


def kernel(x_prompt, x_sample, cache_a_k, cache_a_v, state_win_k, state_win_v, state_conv, page_table, norm_g, w_qkv_a, w_o_a, sb_bias, w_kv_b, kv_norm_g, w_q_b, w_o_b, sinks_b, w_up, conv_w, conv_b, w_down):
    n_seq, seq_len, d = x_prompt.shape
    n_dec = x_sample.shape[0]
    depth = norm_g.shape[0]
    n_a = w_qkv_a.shape[0]
    h_a = sb_bias.shape[1]
    h_b = sinks_b.shape[1]
    d_ff = w_down.shape[1]
    past_len = page_table.shape[1] * PAGE_SIZE
    qa_w = h_a * HD_A
    ka_w = KVH_A * HD_A
    kb_w = KVH_B * HD_B

    bf = lambda a: a.astype(BF16)
    w_qkv_a, w_o_a, w_kv_b, w_q_b, w_o_b, w_up, w_down = map(
        bf, (w_qkv_a, w_o_a, w_kv_b[None], w_q_b, w_o_b, w_up, w_down))

    rope_p = _rope_tables(jnp.arange(seq_len))
    rope_s = _rope_tables(jnp.full((n_dec,), past_len))
    cache_k = cache_a_k.reshape(cache_a_k.shape[:2] + (PAGE_SIZE * KVH_A, HD_A))
    cache_v = cache_a_v.reshape(cache_a_v.shape[:2] + (PAGE_SIZE * KVH_A, HD_A))
    win_k = state_win_k.reshape(n_dec, -1, kb_w)
    win_v = state_win_v.reshape(n_dec, -1, kb_w)

    x = x_prompt.reshape(n_seq * seq_len, d)
    ak_p, av_p, conv_p = [], [], []
    kv = None
    for l in range(depth):
        if l < n_a:
            qkv, k_rows, v_rows = norm_matmul(x, norm_g[l, 0], w_qkv_a, l, tm=256, kv_col0=qa_w, kv_heads=KVH_A,
                                              name=f"p{l}_qkv")
            ak_p.append(k_rows.reshape(n_seq, seq_len, KVH_A, HD_A))
            av_p.append(v_rows.reshape(n_seq, seq_len, KVH_A, HD_A))
            o = sb_prompt(qkv, sb_bias[l], n_seq=n_seq, seq_len=seq_len, name=f"p{l}_sb")
            x = matmul_norm_res(o, w_o_a, l, norm_g[l, 1], x, tm=256, name=f"p{l}_wo")
        else:
            jb = l - n_a
            if kv is None:
                kv = norm_matmul(x, kv_norm_g, w_kv_b, 0, tm=512, rope=rope_p, rope_cols=kb_w, name="p_kv")
            qb = norm_matmul(x, norm_g[l, 0], w_q_b, jb, tm=512, rope=rope_p, rope_cols=h_b * HD_B, name=f"p{l}_q")
            o = swa_prompt(qb, kv, sinks_b[jb], n_seq=n_seq, seq_len=seq_len, name=f"p{l}_swa")
            x = matmul_norm_res(o, w_o_b, jb, norm_g[l, 1], x, tm=256, name=f"p{l}_wo")
        x, cog, cov = ffn(x, norm_g[l, 2], w_up, conv_w[l], conv_b[l], w_down, norm_g[l, 3], l,
                          seq_len=seq_len, tm=FFN_TM, tf=512, chunk=256, name=f"p{l}_ffn")
        last = slice(seq_len // FFN_TM - 1, None, seq_len // FFN_TM)
        conv_p.append(jnp.concatenate([cog[last, SUBLANES - 2:], cov[last, SUBLANES - 2:]], axis=-1))
    y_prompt = x.reshape(n_seq, seq_len, d)
    n_keep = min(WINDOW, seq_len)
    kv_tail = kv.reshape(n_seq, seq_len, 2 * kb_w)[:, seq_len - n_keep:]
    wk_p = kv_tail[..., :kb_w].reshape(n_seq, n_keep, KVH_B, HD_B)
    wv_p = kv_tail[..., kb_w:].reshape(n_seq, n_keep, KVH_B, HD_B)

    x = x_sample.reshape(n_dec, d)
    ak_s, av_s, conv_s = [], [], []
    kv = None
    for l in range(depth):
        if l < n_a:
            qkv, k_rows, v_rows = norm_matmul(x, norm_g[l, 0], w_qkv_a, l, tm=n_dec, kv_col0=qa_w, kv_heads=KVH_A,
                                              name=f"s{l}_qkv")
            ak_s.append(k_rows.reshape(n_dec, 1, KVH_A, HD_A))
            av_s.append(v_rows.reshape(n_dec, 1, KVH_A, HD_A))
            o = sb_decode(qkv[:, :qa_w].reshape(n_dec, h_a, HD_A), sb_bias[l], cache_k, cache_v, page_table,
                          layer=l, pps=8, name=f"s{l}_sb")
            x = matmul_norm_res(o.reshape(n_dec, qa_w), w_o_a, l, norm_g[l, 1], x, tm=n_dec, name=f"s{l}_wo")
        else:
            jb = l - n_a
            if kv is None:
                kv = norm_matmul(x, kv_norm_g, w_kv_b, 0, tm=n_dec, rope=rope_s, rope_cols=kb_w, name="s_kv")
                k_new = kv[:, None, :kb_w]
                v_new = kv[:, None, kb_w:]
            qb = norm_matmul(x, norm_g[l, 0], w_q_b, jb, tm=n_dec, rope=rope_s, rope_cols=h_b * HD_B, name=f"s{l}_q")
            o = swa_decode(qb.reshape(n_dec, h_b, HD_B), sinks_b[jb], win_k, win_v, k_new, v_new, name=f"s{l}_swa")
            x = matmul_norm_res(o.reshape(n_dec, h_b * HD_B), w_o_b, jb, norm_g[l, 1], x, tm=n_dec, name=f"s{l}_wo")
        x, u_new = ffn_step(x, norm_g[l, 2], w_up, conv_w[l], conv_b[l],
                            state_conv[l, :, 0], state_conv[l, :, 1], w_down, norm_g[l, 3], l,
                            tf=512, name=f"s{l}_ffn")
        conv_s.append(jnp.concatenate([state_conv[l, :, 1:], u_new[:, None, :]], axis=1)[:, -(CONV_W - 1):])
    y_sample = x.reshape(n_dec, 1, d)
    n_keep_s = min(WINDOW, win_k.shape[1] + 1)
    wk_s = jnp.concatenate([win_k, k_new], axis=1)[:, -n_keep_s:].reshape(n_dec, n_keep_s, KVH_B, HD_B)
    wv_s = jnp.concatenate([win_v, v_new], axis=1)[:, -n_keep_s:].reshape(n_dec, n_keep_s, KVH_B, HD_B)

    return (y_prompt, y_sample, jnp.stack(ak_p), jnp.stack(av_p), jnp.stack(ak_s), jnp.stack(av_s),
            wk_p, wv_p, wk_s, wv_s, jnp.stack(conv_p), jnp.stack(conv_s))
```

```python
import functools

import jax
import jax.numpy as jnp
from jax import lax
from jax.experimental import pallas as pl
from jax.experimental.pallas import tpu as pltpu

F32 = jnp.float32
BF16 = jnp.bfloat16

EPS = 1e-6
PAGE_SIZE = 128
HD_A = 128
KVH_A = 4
HD_B = 64
KVH_B = 4
WINDOW = 128
ROPE_DIM = HD_B // 4
ROPE_THETA = 500000.0
CONV_W = 3

LANES = 128
SUBLANES = 8
VMEM_LIMIT = 56 * 1024 * 1024
QKV_TM = 256
PROJ_TM = 512
WO_TM = 256
FFN_TM = 512
FFN_TF = 512
FFN_CHUNK = 256
SB_Q_BLOCKS = 2
DECODE_PAGES = 16

_NT = (((1,), (1,)), ((), ()))


def _params(n_axes, vmem=VMEM_LIMIT):
    return pltpu.CompilerParams(dimension_semantics=("arbitrary",) * n_axes, vmem_limit_bytes=vmem)


def _rms(x, g):
    ms = jnp.mean(x * x, axis=-1, keepdims=True)
    return x * lax.rsqrt(ms + EPS) * g


COL_CHUNK = 512


def _norm_matmul_kernel(*refs, rope_cols, kv_col0, kv_heads):
    refs = list(refs)
    x_ref, g_ref, w_ref = refs[:3]
    rope_refs = refs[3:6] if rope_cols else ()
    o_ref = refs[3 + len(rope_refs)]
    kv_refs = refs[4 + len(rope_refs):]
    tm, n = o_ref.shape
    h = _rms(x_ref[...], g_ref[...]).astype(BF16)
    if rope_cols:
        c, sa, sb = (r[...] for r in rope_refs)
    for c0 in range(0, n, COL_CHUNK):
        acc = jnp.dot(h, w_ref[:, c0:c0 + COL_CHUNK], preferred_element_type=F32)
        for t in range(0, min(COL_CHUNK, n - c0), LANES):
            a = acc[:, t:t + LANES]
            col = c0 + t
            if col < rope_cols:
                a = a * c + pltpu.roll(a, LANES - ROPE_DIM // 2, 1) * sa + pltpu.roll(a, ROPE_DIM // 2, 1) * sb
            o_ref[:, col:col + LANES] = a
            if kv_refs and col >= kv_col0:
                which, head = divmod((col - kv_col0) // LANES, kv_heads)
                kv_refs[which][pl.ds(head, tm, stride=kv_heads), :] = a


def norm_matmul(x, g, w, layer, *, tm, rope=None, rope_cols=0, kv_col0=None, kv_heads=0, name):
    m, d = x.shape
    n = w.shape[2]
    in_specs = [
        pl.BlockSpec((tm, d), lambda i: (i, 0)),
        pl.BlockSpec((1, d), lambda i: (0, 0)),
        pl.BlockSpec((None, d, n), lambda i: (layer, 0, 0)),
    ]
    args = [x, g.reshape(1, d), w]
    if rope_cols:
        n_tab = rope[0].shape[0] // tm
        for t in rope:
            in_specs.append(pl.BlockSpec((tm, LANES), lambda i: (i % n_tab, 0)))
            args.append(t)
    out_specs = [pl.BlockSpec((tm, n), lambda i: (i, 0))]
    out_shape = [jax.ShapeDtypeStruct((m, n), F32)]
    if kv_col0 is not None:
        out_specs += [pl.BlockSpec((tm * kv_heads, LANES), lambda i: (i, 0))] * 2
        out_shape += [jax.ShapeDtypeStruct((m * kv_heads, LANES), F32)] * 2
    out = pl.pallas_call(
        functools.partial(_norm_matmul_kernel, rope_cols=rope_cols, kv_col0=kv_col0, kv_heads=kv_heads),
        grid=(m // tm,),
        in_specs=in_specs,
        out_specs=out_specs,
        out_shape=out_shape,
        compiler_params=_params(1),
        name=name,
    )(*args)
    return out if kv_col0 is not None else out[0]


def _matmul_norm_res_kernel(a_ref, w_ref, g_ref, x_ref, o_ref):
    m = jnp.dot(a_ref[...].astype(BF16), w_ref[...], preferred_element_type=F32)
    o_ref[...] = x_ref[...] + _rms(m, g_ref[...])


def matmul_norm_res(a, w, layer, g, x, *, tm, name):
    m, k = a.shape
    n = w.shape[2]
    return pl.pallas_call(
        _matmul_norm_res_kernel,
        grid=(m // tm,),
        in_specs=[
            pl.BlockSpec((tm, k), lambda i: (i, 0)),
            pl.BlockSpec((None, k, n), lambda i: (layer, 0, 0)),
            pl.BlockSpec((1, n), lambda i: (0, 0)),
            pl.BlockSpec((tm, n), lambda i: (i, 0)),
        ],
        out_specs=pl.BlockSpec((tm, n), lambda i: (i, 0)),
        out_shape=jax.ShapeDtypeStruct((m, n), F32),
        compiler_params=_params(1),
        name=name,
    )(a, w, g.reshape(1, n), x)


def _ffn_kernel(x_ref, g2_ref, wg_ref, wv_ref, cwg_ref, cwv_ref, cbg_ref, cbv_ref, wd_ref, g3_ref,
                o_ref, cog_ref, cov_ref,
                h_ref, acc_ref, carg_ref, carv_ref, *buf_refs, tiles_per_seq):
    i = pl.program_id(0)
    j = pl.program_id(1)
    tm = x_ref.shape[0]
    n_chunks = len(buf_refs) // 2
    bufg_refs, bufv_refs = buf_refs[:n_chunks], buf_refs[n_chunks:]
    chunk = bufg_refs[0].shape[1]
    col = lambda c: slice(c * chunk, (c + 1) * chunk)

    @pl.when(j == 0)
    def _():
        h_ref[...] = _rms(x_ref[...], g2_ref[...]).astype(BF16)
        acc_ref[...] = jnp.zeros_like(acc_ref)

    seq_start = (i % tiles_per_seq) == 0

    @pl.when(seq_start)
    def _():
        for buf_ref in buf_refs:
            buf_ref[0:SUBLANES, :] = jnp.zeros((SUBLANES, chunk), F32)

    @pl.when(jnp.logical_not(seq_start))
    def _():
        for c in range(n_chunks):
            bufg_refs[c][0:SUBLANES, :] = carg_ref[j, :, col(c)]
            bufv_refs[c][0:SUBLANES, :] = carv_ref[j, :, col(c)]

    for c in range(n_chunks):
        bufg_refs[c][SUBLANES:SUBLANES + tm, :] = jnp.dot(h_ref[...], wg_ref[:, col(c)], preferred_element_type=F32)
        bufv_refs[c][SUBLANES:SUBLANES + tm, :] = jnp.dot(h_ref[...], wv_ref[:, col(c)], preferred_element_type=F32)

    def conv(c, cw_ref, cb_ref, buf_ref, car_ref, co_ref):
        tail = buf_ref[tm:tm + SUBLANES, :]
        car_ref[j, :, col(c)] = tail
        co_ref[0, :, col(c)] = tail
        cw = cw_ref[:, col(c)]
        y = cb_ref[:, col(c)] + cw[0:1] * buf_ref[SUBLANES - 2:SUBLANES - 2 + tm, :]
        y = y + cw[1:2] * buf_ref[SUBLANES - 1:SUBLANES - 1 + tm, :]
        return y + cw[2:3] * buf_ref[SUBLANES:SUBLANES + tm, :]

    for c in range(n_chunks):
        cg = conv(c, cwg_ref, cbg_ref, bufg_refs[c], carg_ref, cog_ref)
        cv = conv(c, cwv_ref, cbv_ref, bufv_refs[c], carv_ref, cov_ref)
        act = (jax.nn.gelu(cg, approximate=True) * cv).astype(BF16)
        acc_ref[...] += jnp.dot(act, wd_ref[col(c), :], preferred_element_type=F32)

    @pl.when(j == pl.num_programs(1) - 1)
    def _():
        o_ref[...] = x_ref[...] + _rms(acc_ref[...], g3_ref[...])


def ffn(x, g2, w_up, conv_w, conv_b, w_down, g3, layer, *, seq_len, tm, tf, chunk, name):
    m, d = x.shape
    d_ff = w_down.shape[1]
    nj = d_ff // tf
    tiles_per_seq = seq_len // tm
    conv_b2 = conv_b.reshape(1, 2 * d_ff)
    out, cog, cov = pl.pallas_call(
        functools.partial(_ffn_kernel, tiles_per_seq=tiles_per_seq),
        grid=(m // tm, nj),
        in_specs=[
            pl.BlockSpec((tm, d), lambda i, j: (i, 0)),
            pl.BlockSpec((1, d), lambda i, j: (0, 0)),
            pl.BlockSpec((None, d, tf), lambda i, j: (layer, 0, j)),
            pl.BlockSpec((None, d, tf), lambda i, j: (layer, 0, j + nj)),
            pl.BlockSpec((CONV_W, tf), lambda i, j: (0, j)),
            pl.BlockSpec((CONV_W, tf), lambda i, j: (0, j + nj)),
            pl.BlockSpec((1, tf), lambda i, j: (0, j)),
            pl.BlockSpec((1, tf), lambda i, j: (0, j + nj)),
            pl.BlockSpec((None, tf, d), lambda i, j: (layer, j, 0)),
            pl.BlockSpec((1, d), lambda i, j: (0, 0)),
        ],
        out_specs=[
            pl.BlockSpec((tm, d), lambda i, j: (i, 0)),
            pl.BlockSpec((1, SUBLANES, tf), lambda i, j: (i, 0, j)),
            pl.BlockSpec((1, SUBLANES, tf), lambda i, j: (i, 0, j)),
        ],
        out_shape=[
            jax.ShapeDtypeStruct((m, d), F32),
            jax.ShapeDtypeStruct((m // tm, SUBLANES, d_ff), F32),
            jax.ShapeDtypeStruct((m // tm, SUBLANES, d_ff), F32),
        ],
        scratch_shapes=[
            pltpu.VMEM((tm, d), BF16),
            pltpu.VMEM((tm, d), F32),
            pltpu.VMEM((nj, SUBLANES, tf), F32),
            pltpu.VMEM((nj, SUBLANES, tf), F32),
        ] + [pltpu.VMEM((tm + SUBLANES, chunk), F32)] * (2 * (tf // chunk)),
        compiler_params=_params(2),
        name=name,
    )(x, g2.reshape(1, d), w_up, w_up, conv_w, conv_w, conv_b2, conv_b2, w_down, g3.reshape(1, d))
    return out, cog, cov


def _ffn_step_kernel(x_ref, g2_ref, wg_ref, wv_ref, cwg_ref, cwv_ref, cbg_ref, cbv_ref,
                     p0g_ref, p0v_ref, p1g_ref, p1v_ref, wd_ref, g3_ref,
                     o_ref, ug_ref, uv_ref, h_ref, acc_ref):
    j = pl.program_id(0)

    @pl.when(j == 0)
    def _():
        h_ref[...] = _rms(x_ref[...], g2_ref[...]).astype(BF16)
        acc_ref[...] = jnp.zeros_like(acc_ref)

    h = h_ref[...]

    def conv(w_ref, cw_ref, cb_ref, p0_ref, p1_ref, u_ref):
        u = jnp.dot(h, w_ref[...], preferred_element_type=F32)
        u_ref[...] = u
        cw = cw_ref[...]
        c = cb_ref[...] + cw[0:1] * p0_ref[...]
        c = c + cw[1:2] * p1_ref[...]
        return c + cw[2:3] * u

    cg = conv(wg_ref, cwg_ref, cbg_ref, p0g_ref, p1g_ref, ug_ref)
    cv = conv(wv_ref, cwv_ref, cbv_ref, p0v_ref, p1v_ref, uv_ref)
    act = (jax.nn.gelu(cg, approximate=True) * cv).astype(BF16)
    acc_ref[...] += jnp.dot(act, wd_ref[...], preferred_element_type=F32)

    @pl.when(j == pl.num_programs(0) - 1)
    def _():
        o_ref[...] = x_ref[...] + _rms(acc_ref[...], g3_ref[...])


def ffn_step(x, g2, w_up, conv_w, conv_b, prev0, prev1, w_down, g3, layer, *, tf, name):
    m, d = x.shape
    d_ff = w_down.shape[1]
    nj = d_ff // tf
    conv_b2 = conv_b.reshape(1, 2 * d_ff)
    lo = lambda j: (0, j)
    hi = lambda j: (0, j + nj)
    out, ug, uv = pl.pallas_call(
        _ffn_step_kernel,
        grid=(nj,),
        in_specs=[
            pl.BlockSpec((m, d), lambda j: (0, 0)),
            pl.BlockSpec((1, d), lambda j: (0, 0)),
            pl.BlockSpec((None, d, tf), lambda j: (layer, 0, j)),
            pl.BlockSpec((None, d, tf), lambda j: (layer, 0, j + nj)),
            pl.BlockSpec((CONV_W, tf), lo),
            pl.BlockSpec((CONV_W, tf), hi),
            pl.BlockSpec((1, tf), lo),
            pl.BlockSpec((1, tf), hi),
            pl.BlockSpec((m, tf), lo),
            pl.BlockSpec((m, tf), hi),
            pl.BlockSpec((m, tf), lo),
            pl.BlockSpec((m, tf), hi),
            pl.BlockSpec((None, tf, d), lambda j: (layer, j, 0)),
            pl.BlockSpec((1, d), lambda j: (0, 0)),
        ],
        out_specs=[
            pl.BlockSpec((m, d), lambda j: (0, 0)),
            pl.BlockSpec((m, tf), lo),
            pl.BlockSpec((m, tf), lo),
        ],
        out_shape=[
            jax.ShapeDtypeStruct((m, d), F32),
            jax.ShapeDtypeStruct((m, d_ff), F32),
            jax.ShapeDtypeStruct((m, d_ff), F32),
        ],
        scratch_shapes=[pltpu.VMEM((m, d), BF16), pltpu.VMEM((m, d), F32)],
        compiler_params=_params(1),
        name=name,
    )(x, g2.reshape(1, d), w_up, w_up, conv_w, conv_w, conv_b2, conv_b2,
      prev0, prev0, prev1, prev1, w_down, g3.reshape(1, d))
    return out, jnp.concatenate([ug, uv], axis=-1)


def _suffix_sum_matrix():
    j = lax.broadcasted_iota(jnp.int32, (2 * LANES, 2 * LANES), 0) % LANES
    s = lax.broadcasted_iota(jnp.int32, (2 * LANES, 2 * LANES), 1)
    return jnp.where((s >= LANES) | (j > s), 1.0, 0.0).astype(BF16)


def _sb_terms(z, valid):
    sp = jnp.maximum(z, 0.0) + jnp.log(1.0 + jnp.exp(-jnp.abs(z)))
    log_beta = z - sp
    if valid is not None:
        sp = jnp.where(valid, sp, 0.0)
    return sp, log_beta


def _hi_lo(x):
    hi = x.astype(BF16)
    return jnp.concatenate([hi, (x - hi.astype(F32)).astype(BF16)], axis=1)


def _sb_weight(log_beta, cum, car, valid):
    w = jnp.exp(log_beta - cum[:, :LANES] - car)
    if valid is not None:
        w = jnp.where(valid, w, 0.0)
    return w.astype(BF16), car + cum[:, LANES:]


def _sb_prompt_kernel(bias_ref, q_ref, k_ref, v_ref, w2_ref, o_ref, acc_ref, car_ref, bias_v_ref, *,
                      scale, group, row_splits, sweep_blocks, q_blocks):
    kvh = pl.program_id(1)
    tq = q_ref.shape[0] // q_blocks
    rows = group * tq
    split = rows // row_splits
    w2 = w2_ref[...]
    for g in range(group):
        bias_v_ref[g * tq:(g + 1) * tq, :] = jnp.full((tq, LANES), bias_ref[kvh * group + g], F32)
    t_idx = lax.broadcasted_iota(jnp.int32, (split, tq), 0) % tq
    s_idx = lax.broadcasted_iota(jnp.int32, (split, tq), 1)
    rsl = lambda r: slice(r * split, (r + 1) * split)

    def sweep(qs, start, n_blocks, diagonal):
        top_valid = s_idx < jnp.where(diagonal, t_idx, tq)
        valid_of = lambda b: top_valid if b == n_blocks - 1 else None
        st = start if isinstance(start, int) else pl.multiple_of(start, tq)
        kb = k_ref[pl.ds(st, n_blocks * tq), :].astype(BF16)
        vb = v_ref[pl.ds(st, n_blocks * tq), :].astype(BF16)
        order = list(reversed(range(n_blocks)))
        zs = [lax.dot_general(qs[rsl(r)], kb, _NT, preferred_element_type=F32) for r in range(row_splits)]
        log_betas, cums = [], []
        for r in range(row_splits):
            bias = bias_v_ref[rsl(r)]
            sps, lbs = [], {}
            for b in order:
                sp, lbs[b] = _sb_terms(zs[r][:, b * tq:(b + 1) * tq] * scale + bias, valid_of(b))
                sps.append(_hi_lo(sp))
            log_betas.append(lbs)
            cums.append(jnp.dot(jnp.concatenate(sps, axis=0), w2, preferred_element_type=F32))
        for r in range(row_splits):
            car = car_ref[rsl(r)]
            ws = {}
            for t, b in enumerate(order):
                ws[b], car = _sb_weight(log_betas[r][b], cums[r][t * split:(t + 1) * split], car, valid_of(b))
            car_ref[rsl(r)] = car
            w_all = jnp.concatenate([ws[b] for b in range(n_blocks)], axis=1)
            acc_ref[rsl(r)] += jnp.dot(w_all, vb, preferred_element_type=F32)

    def query_block(u, carry):
        i = pl.program_id(2) * q_blocks + u
        row0 = pl.multiple_of(u * tq, tq)
        q = q_ref[pl.ds(row0, tq), :]
        qs = jnp.concatenate([q[:, g * HD_A:(g + 1) * HD_A] for g in range(group)], axis=0).astype(BF16)
        acc_ref[...] = jnp.zeros_like(acc_ref)
        car_ref[...] = jnp.zeros_like(car_ref)

        n_full = (i + 1) // sweep_blocks

        def full(p, c):
            sweep(qs, (i + 1 - sweep_blocks * (p + 1)) * tq, sweep_blocks, p == 0)
            return c

        lax.fori_loop(0, n_full, full, 0)
        for rem in range(1, sweep_blocks):
            @pl.when((i + 1) % sweep_blocks == rem)
            def _():
                sweep(qs, 0, rem, n_full == 0)

        for g in range(group):
            o_ref[pl.ds(row0, tq), g * HD_A:(g + 1) * HD_A] = acc_ref[g * tq:(g + 1) * tq].astype(o_ref.dtype)
        return carry

    lax.fori_loop(0, q_blocks, query_block, 0)


def sb_prompt(qkv, bias, *, n_seq, seq_len, q_blocks, name):
    m = qkv.shape[0]
    h_a = bias.shape[0]
    group = h_a // KVH_A
    tq = PAGE_SIZE * q_blocks
    nq = seq_len // tq
    qw = group * HD_A
    k_col0 = h_a
    v_col0 = k_col0 + KVH_A
    return pl.pallas_call(
        functools.partial(_sb_prompt_kernel, scale=HD_A ** -0.5, group=group, row_splits=2, sweep_blocks=4,
                          q_blocks=q_blocks),
        grid=(n_seq, KVH_A, nq),
        in_specs=[
            pl.BlockSpec(memory_space=pltpu.SMEM),
            pl.BlockSpec((tq, qw), lambda b, h, i: (b * nq + i, h)),
            pl.BlockSpec((seq_len, HD_A), lambda b, h, i: (b, k_col0 + h)),
            pl.BlockSpec((seq_len, HD_A), lambda b, h, i: (b, v_col0 + h)),
            pl.BlockSpec((2 * LANES, 2 * LANES), lambda b, h, i: (0, 0)),
        ],
        out_specs=pl.BlockSpec((tq, qw), lambda b, h, i: (b * nq + i, h)),
        out_shape=jax.ShapeDtypeStruct((m, h_a * HD_A), BF16),
        scratch_shapes=[pltpu.VMEM((group * PAGE_SIZE, HD_A), F32), pltpu.VMEM((group * PAGE_SIZE, LANES), F32),
                        pltpu.VMEM((group * PAGE_SIZE, LANES), F32)],
        compiler_params=_params(3),
        name=name,
    )(bias, qkv, qkv, qkv, _suffix_sum_matrix())


def _sb_decode_kernel(pt_ref, q_ref, bias_ref, w2_ref, *refs, scale, pps, group):
    k_refs = refs[:pps]
    v_refs = refs[pps:2 * pps]
    o_ref, acc_ref, car_ref = refs[2 * pps:]
    s = pl.program_id(1)

    @pl.when(s == 0)
    def _():
        acc_ref[...] = jnp.zeros_like(acc_ref)
        car_ref[...] = jnp.zeros_like(car_ref)

    q = q_ref[0].astype(BF16)
    n_chunks = k_refs[0].shape[0] // LANES
    row_kvh = lax.broadcasted_iota(jnp.int32, (q.shape[0], LANES), 0) // group
    col_kvh = lax.broadcasted_iota(jnp.int32, (q.shape[0], LANES), 1) % KVH_A
    own = row_kvh == col_kvh
    bias = bias_ref[...]
    terms = []
    for r in range(pps):
        kp = k_refs[r][...].astype(BF16)
        z = lax.dot_general(q, kp, _NT, preferred_element_type=F32) * scale
        for c in reversed(range(n_chunks)):
            terms.append(_sb_terms(z[:, c * LANES:(c + 1) * LANES] + bias, own))
    n_heads = q.shape[0]
    cum = jnp.dot(jnp.concatenate([_hi_lo(sp) for sp, _ in terms], axis=0), w2_ref[...], preferred_element_type=F32)
    car = car_ref[...]
    ws = []
    for t, (_, log_beta) in enumerate(terms):
        w, car = _sb_weight(log_beta, cum[t * n_heads:(t + 1) * n_heads], car, own)
        ws.append(w)
    car_ref[...] = car
    acc = acc_ref[...]
    for r in range(pps):
        w_page = jnp.concatenate([ws[r * n_chunks + (n_chunks - 1 - c)] for c in range(n_chunks)], axis=1)
        acc = acc + jnp.dot(w_page, v_refs[r][...].astype(BF16), preferred_element_type=F32)
    acc_ref[...] = acc

    @pl.when(s == pl.num_programs(1) - 1)
    def _():
        o_ref[0] = acc_ref[...]


def sb_decode(q, bias, cache_k, cache_v, page_table, *, layer, pps, name):
    n, h_a, _ = q.shape
    group = h_a // KVH_A
    n_pages = page_table.shape[1]
    page_rows = cache_k.shape[2]

    def page_spec(r):
        def index_map(b, s, pt):
            return (layer, pt[b * n_pages + (n_pages - 1 - (s * pps + r))], 0, 0)
        return pl.BlockSpec((None, None, page_rows, HD_A), index_map)

    grid_spec = pltpu.PrefetchScalarGridSpec(
        num_scalar_prefetch=1,
        grid=(n, n_pages // pps),
        in_specs=[
            pl.BlockSpec((1, h_a, HD_A), lambda b, s, pt: (b, 0, 0)),
            pl.BlockSpec((h_a, LANES), lambda b, s, pt: (0, 0)),
            pl.BlockSpec((2 * LANES, 2 * LANES), lambda b, s, pt: (0, 0)),
        ] + [page_spec(r) for r in range(pps)] * 2,
        out_specs=pl.BlockSpec((1, h_a, HD_A), lambda b, s, pt: (b, 0, 0)),
        scratch_shapes=[pltpu.VMEM((h_a, HD_A), F32), pltpu.VMEM((h_a, LANES), F32)],
    )
    bias_b = jnp.broadcast_to(bias.astype(F32)[:, None], (h_a, LANES))
    return pl.pallas_call(
        functools.partial(_sb_decode_kernel, scale=HD_A ** -0.5, pps=pps, group=group),
        grid_spec=grid_spec,
        out_shape=jax.ShapeDtypeStruct((n, h_a, HD_A), F32),
        compiler_params=_params(2),
        name=name,
    )(page_table.reshape(-1), q, bias_b, _suffix_sum_matrix(), *([cache_k] * pps), *([cache_v] * pps))


def _swa_prompt_kernel(sink_ref, q_ref, kc_ref, kp_ref, vc_ref, vp_ref, o_ref, *, scale, n_heads):
    i = pl.program_id(1)
    tq = q_ref.shape[0]
    half = HD_B
    lane = lax.broadcasted_iota(jnp.int32, (tq, LANES), 1)
    low = lane < half
    row = lax.broadcasted_iota(jnp.int32, (tq, tq), 0)
    col = lax.broadcasted_iota(jnp.int32, (tq, tq), 1)
    mask_cur = col <= row
    mask_prev = (col >= row) & (i > 0)
    mask = jnp.concatenate([mask_prev, mask_cur], axis=1)
    mask2 = jnp.concatenate([mask, mask], axis=0)
    low2 = jnp.concatenate([low, low], axis=0)
    group = n_heads // KVH_B
    for c in range(KVH_B // 2):
        kk = jnp.concatenate([kp_ref[:, c * LANES:(c + 1) * LANES], kc_ref[:, c * LANES:(c + 1) * LANES]], axis=0)
        vv = jnp.concatenate([vp_ref[:, c * LANES:(c + 1) * LANES], vc_ref[:, c * LANES:(c + 1) * LANES]], axis=0)
        kr = pltpu.roll(kk, half, 1)
        vr = pltpu.roll(vv, half, 1)
        for e in range(2):
            kvh = 2 * c + e
            kd = (jnp.where(low2, kk, kr) if e == 0 else jnp.where(low2, kr, kk)).astype(BF16)
            vd = (jnp.where(low2, vv, vr) if e == 0 else jnp.where(low2, vr, vv)).astype(BF16)
            for p in range(group // 2):
                h0 = kvh * group + 2 * p
                qcol = h0 // 2
                qp = q_ref[:, qcol * LANES:(qcol + 1) * LANES]
                q2 = jnp.concatenate([jnp.where(low, qp, 0.0), jnp.where(low, 0.0, qp)], axis=0).astype(BF16)
                s = lax.dot_general(q2, kd, _NT, preferred_element_type=F32) * scale
                s = jnp.where(mask2, s, -jnp.inf)
                rid = lax.broadcasted_iota(jnp.int32, (2 * tq, 1), 0)
                sink = jnp.where(rid < tq, sink_ref[h0], sink_ref[h0 + 1])
                m = jnp.maximum(jnp.max(s, axis=-1, keepdims=True), sink)
                pexp = jnp.exp(s - m)
                denom = jnp.sum(pexp, axis=-1, keepdims=True) + jnp.exp(sink - m)
                o2 = jnp.dot(pexp.astype(BF16), vd, preferred_element_type=F32) / denom
                o_ref[:, qcol * LANES:(qcol + 1) * LANES] = jnp.where(low, o2[:tq], o2[tq:]).astype(o_ref.dtype)


def swa_prompt(q, kv, sinks, *, n_seq, seq_len, name):
    m, qw = q.shape
    n_heads = sinks.shape[0]
    tq = WINDOW
    nq = seq_len // tq
    kw = KVH_B * HD_B
    cur = lambda col: (lambda b, i: (b * nq + i, col))
    prev = lambda col: (lambda b, i: (b * nq + jnp.maximum(i - 1, 0), col))
    return pl.pallas_call(
        functools.partial(_swa_prompt_kernel, scale=HD_B ** -0.5, n_heads=n_heads),
        grid=(n_seq, nq),
        in_specs=[
            pl.BlockSpec(memory_space=pltpu.SMEM),
            pl.BlockSpec((tq, qw), lambda b, i: (b * nq + i, 0)),
            pl.BlockSpec((tq, kw), cur(0)),
            pl.BlockSpec((tq, kw), prev(0)),
            pl.BlockSpec((tq, kw), cur(1)),
            pl.BlockSpec((tq, kw), prev(1)),
        ],
        out_specs=pl.BlockSpec((tq, qw), lambda b, i: (b * nq + i, 0)),
        out_shape=jax.ShapeDtypeStruct((m, qw), BF16),
        compiler_params=_params(2),
        name=name,
    )(sinks, q, kv, kv, kv, kv)


def _swa_decode_kernel(q_ref, sink_ref, kw_ref, vw_ref, kn_ref, vn_ref, o_ref, *, scale, group):
    q = q_ref[0]
    row_kvh = lax.broadcasted_iota(jnp.int32, q.shape, 0) // group
    lane_kvh = lax.broadcasted_iota(jnp.int32, q.shape, 1) // HD_B
    own = row_kvh == lane_kvh
    qbd = jnp.where(own, q, 0.0)
    qb = qbd.astype(BF16)
    kwin = kw_ref[0].astype(BF16)
    vwin = vw_ref[0].astype(BF16)
    kn = kn_ref[0]
    vn = vn_ref[0]
    s = lax.dot_general(qb, kwin, _NT, preferred_element_type=F32) * scale
    s_new = jnp.sum(qb.astype(F32) * kn.astype(BF16).astype(F32), axis=-1, keepdims=True) * scale
    sink = sink_ref[:, 0:1]
    m = jnp.maximum(jnp.maximum(jnp.max(s, axis=-1, keepdims=True), s_new), sink)
    p = jnp.exp(s - m)
    p_new = jnp.exp(s_new - m)
    denom = jnp.sum(p, axis=-1, keepdims=True) + p_new + jnp.exp(sink - m)
    o = jnp.dot(p.astype(BF16), vwin, preferred_element_type=F32)
    o = o + p_new.astype(BF16).astype(F32) * vn.astype(BF16).astype(F32)
    o = jnp.where(own, o / denom, 0.0)
    o = o + pltpu.roll(o, 2 * HD_B, 1)
    o = o + pltpu.roll(o, HD_B, 1)
    o_ref[0] = o[:, :HD_B]


def swa_decode(q, sinks, win_k, win_v, k_new, v_new, *, name):
    n, n_heads, _ = q.shape
    w = win_k.shape[1]
    kw = KVH_B * HD_B
    q_t = jnp.tile(q, (1, 1, KVH_B))
    sink_b = jnp.broadcast_to(sinks.astype(F32)[:, None], (n_heads, LANES))
    per_seq = lambda a, b: pl.BlockSpec((1, a, b), lambda s: (s, 0, 0))
    return pl.pallas_call(
        functools.partial(_swa_decode_kernel, scale=HD_B ** -0.5, group=n_heads // KVH_B),
        grid=(n,),
        in_specs=[
            per_seq(n_heads, kw),
            pl.BlockSpec((n_heads, LANES), lambda s: (0, 0)),
            per_seq(w, kw), per_seq(w, kw), per_seq(1, kw), per_seq(1, kw),
        ],
        out_specs=per_seq(n_heads, HD_B),
        out_shape=jax.ShapeDtypeStruct((n, n_heads, HD_B), F32),
        compiler_params=_params(1),
        name=name,
    )(q_t, sink_b, win_k, win_v, k_new, v_new)


def _rope_tables(pos):
    n = pos.shape[0]
    half = ROPE_DIM // 2
    inv_freq = 1.0 / (ROPE_THETA ** (jnp.arange(half, dtype=F32) * (2.0 / ROPE_DIM)))
    ang = pos.astype(F32)[:, None] * inv_freq[None, :]
    cos, sin = jnp.cos(ang), jnp.sin(ang)
    pad = lambda a, left, fill: jnp.concatenate(
        [jnp.full((n, left), fill, F32), a, jnp.full((n, HD_B - left - a.shape[1]), fill, F32)], axis=1)
    c = jnp.concatenate([cos, cos, jnp.ones((n, HD_B - ROPE_DIM), F32)], axis=1)
    sa = pad(-sin, 0, 0.0)
    sb = pad(sin, half, 0.0)
    two = lambda a: jnp.concatenate([a, a], axis=1)
    return two(c), two(sa), two(sb)


def kernel(x_prompt, x_sample, cache_a_k, cache_a_v, state_win_k, state_win_v, state_conv, page_table, norm_g, w_qkv_a, w_o_a, sb_bias, w_kv_b, kv_norm_g, w_q_b, w_o_b, sinks_b, w_up, conv_w, conv_b, w_down):
    n_seq, seq_len, d = x_prompt.shape
    n_dec = x_sample.shape[0]
    depth = norm_g.shape[0]
    n_a = w_qkv_a.shape[0]
    h_a = sb_bias.shape[1]
    h_b = sinks_b.shape[1]
    d_ff = w_down.shape[1]
    past_len = page_table.shape[1] * PAGE_SIZE
    qa_w = h_a * HD_A
    ka_w = KVH_A * HD_A
    kb_w = KVH_B * HD_B

    bf = lambda a: a.astype(BF16)
    w_qkv_a, w_o_a, w_kv_b, w_q_b, w_o_b, w_up, w_down = map(
        bf, (w_qkv_a, w_o_a, w_kv_b[None], w_q_b, w_o_b, w_up, w_down))

    rope_p = _rope_tables(jnp.arange(seq_len))
    rope_s = _rope_tables(jnp.full((n_dec,), past_len))
    cache_k = cache_a_k.reshape(cache_a_k.shape[:2] + (PAGE_SIZE * KVH_A, HD_A))
    cache_v = cache_a_v.reshape(cache_a_v.shape[:2] + (PAGE_SIZE * KVH_A, HD_A))
    win_k = state_win_k.reshape(n_dec, -1, kb_w)
    win_v = state_win_v.reshape(n_dec, -1, kb_w)

    x = x_prompt.reshape(n_seq * seq_len, d)
    ak_p, av_p, conv_p = [], [], []
    kv = None
    for l in range(depth):
        if l < n_a:
            qkv, k_rows, v_rows = norm_matmul(x, norm_g[l, 0], w_qkv_a, l, tm=QKV_TM, kv_col0=qa_w, kv_heads=KVH_A,
                                              name=f"p{l}_qkv")
            ak_p.append(k_rows.reshape(n_seq, seq_len, KVH_A, HD_A))
            av_p.append(v_rows.reshape(n_seq, seq_len, KVH_A, HD_A))
            o = sb_prompt(qkv, sb_bias[l], n_seq=n_seq, seq_len=seq_len, q_blocks=SB_Q_BLOCKS, name=f"p{l}_sb")
            x = matmul_norm_res(o, w_o_a, l, norm_g[l, 1], x, tm=WO_TM, name=f"p{l}_wo")
        else:
            jb = l - n_a
            if kv is None:
                kv = norm_matmul(x, kv_norm_g, w_kv_b, 0, tm=PROJ_TM, rope=rope_p, rope_cols=kb_w, name="p_kv")
            qb = norm_matmul(x, norm_g[l, 0], w_q_b, jb, tm=PROJ_TM, rope=rope_p, rope_cols=h_b * HD_B, name=f"p{l}_q")
            o = swa_prompt(qb, kv, sinks_b[jb], n_seq=n_seq, seq_len=seq_len, name=f"p{l}_swa")
            x = matmul_norm_res(o, w_o_b, jb, norm_g[l, 1], x, tm=WO_TM, name=f"p{l}_wo")
        x, cog, cov = ffn(x, norm_g[l, 2], w_up, conv_w[l], conv_b[l], w_down, norm_g[l, 3], l,
                          seq_len=seq_len, tm=FFN_TM, tf=FFN_TF, chunk=FFN_CHUNK, name=f"p{l}_ffn")
        last = slice(seq_len // FFN_TM - 1, None, seq_len // FFN_TM)
        conv_p.append(jnp.concatenate([cog[last, SUBLANES - 2:], cov[last, SUBLANES - 2:]], axis=-1))
    y_prompt = x.reshape(n_seq, seq_len, d)
    n_keep = min(WINDOW, seq_len)
    kv_tail = kv.reshape(n_seq, seq_len, 2 * kb_w)[:, seq_len - n_keep:]
    wk_p = kv_tail[..., :kb_w].reshape(n_seq, n_keep, KVH_B, HD_B)
    wv_p = kv_tail[..., kb_w:].reshape(n_seq, n_keep, KVH_B, HD_B)

    x = x_sample.reshape(n_dec, d)
    ak_s, av_s, conv_s = [], [], []
    kv = None
    for l in range(depth):
        if l < n_a:
            qkv, k_rows, v_rows = norm_matmul(x, norm_g[l, 0], w_qkv_a, l, tm=n_dec, kv_col0=qa_w, kv_heads=KVH_A,
                                              name=f"s{l}_qkv")
            ak_s.append(k_rows.reshape(n_dec, 1, KVH_A, HD_A))
            av_s.append(v_rows.reshape(n_dec, 1, KVH_A, HD_A))
            o = sb_decode(qkv[:, :qa_w].reshape(n_dec, h_a, HD_A), sb_bias[l], cache_k, cache_v, page_table,
                          layer=l, pps=DECODE_PAGES, name=f"s{l}_sb")
            x = matmul_norm_res(o.reshape(n_dec, qa_w), w_o_a, l, norm_g[l, 1], x, tm=n_dec, name=f"s{l}_wo")
        else:
            jb = l - n_a
            if kv is None:
                kv = norm_matmul(x, kv_norm_g, w_kv_b, 0, tm=n_dec, rope=rope_s, rope_cols=kb_w, name="s_kv")
                k_new = kv[:, None, :kb_w]
                v_new = kv[:, None, kb_w:]
            qb = norm_matmul(x, norm_g[l, 0], w_q_b, jb, tm=n_dec, rope=rope_s, rope_cols=h_b * HD_B, name=f"s{l}_q")
            o = swa_decode(qb.reshape(n_dec, h_b, HD_B), sinks_b[jb], win_k, win_v, k_new, v_new, name=f"s{l}_swa")
            x = matmul_norm_res(o.reshape(n_dec, h_b * HD_B), w_o_b, jb, norm_g[l, 1], x, tm=n_dec, name=f"s{l}_wo")
        x, u_new = ffn_step(x, norm_g[l, 2], w_up, conv_w[l], conv_b[l],
                            state_conv[l, :, 0], state_conv[l, :, 1], w_down, norm_g[l, 3], l,
                            tf=FFN_TF, name=f"s{l}_ffn")
        conv_s.append(jnp.concatenate([state_conv[l, :, 1:], u_new[:, None, :]], axis=1)[:, -(CONV_W - 1):])
    y_sample = x.reshape(n_dec, 1, d)
    n_keep_s = min(WINDOW, win_k.shape[1] + 1)
    wk_s = jnp.concatenate([win_k, k_new], axis=1)[:, -n_keep_s:].reshape(n_dec, n_keep_s, KVH_B, HD_B)
    wv_s = jnp.concatenate([win_v, v_new], axis=1)[:, -n_keep_s:].reshape(n_dec, n_keep_s, KVH_B, HD_B)

    return (y_prompt, y_sample, jnp.stack(ak_p), jnp.stack(av_p), jnp.stack(ak_s), jnp.stack(av_s),
            wk_p, wv_p, wk_s, wv_s, jnp.stack(conv_p), jnp.stack(conv_s))
```

```python
import functools

import jax
import jax.numpy as jnp
from jax import lax
from jax.experimental import pallas as pl
from jax.experimental.pallas import tpu as pltpu

F32 = jnp.float32
BF16 = jnp.bfloat16

EPS = 1e-6
PAGE_SIZE = 128
HD_A = 128
KVH_A = 4
HD_B = 64
KVH_B = 4
WINDOW = 128
ROPE_DIM = HD_B // 4
ROPE_THETA = 500000.0
CONV_W = 3

LANES = 128
SUBLANES = 8
VMEM_LIMIT = 56 * 1024 * 1024
QKV_TM = 256
PROJ_TM = 512
WO_TM = 512
FFN_TM = 512
FFN_TF = 512
FFN_CHUNK = 256
FFN_STEP_TF = 1408
SB_Q_BLOCKS = 4
DECODE_PAGES = 32

_NT = (((1,), (1,)), ((), ()))


def _params(n_axes, vmem=VMEM_LIMIT):
    return pltpu.CompilerParams(dimension_semantics=("arbitrary",) * n_axes, vmem_limit_bytes=vmem)


def _rms(x, g):
    ms = jnp.mean(x * x, axis=-1, keepdims=True)
    return x * lax.rsqrt(ms + EPS) * g


COL_CHUNK = 512


def _norm_matmul_kernel(*refs, rope_cols, kv_col0, kv_heads):
    refs = list(refs)
    x_ref, g_ref, w_ref = refs[:3]
    rope_refs = refs[3:6] if rope_cols else ()
    o_ref = refs[3 + len(rope_refs)]
    kv_refs = refs[4 + len(rope_refs):]
    tm, n = o_ref.shape
    h = _rms(x_ref[...], g_ref[...]).astype(BF16)
    if rope_cols:
        c, sa, sb = (r[...] for r in rope_refs)
    for c0 in range(0, n, COL_CHUNK):
        acc = jnp.dot(h, w_ref[:, c0:c0 + COL_CHUNK], preferred_element_type=F32)
        for t in range(0, min(COL_CHUNK, n - c0), LANES):
            a = acc[:, t:t + LANES]
            col = c0 + t
            if col < rope_cols:
                a = a * c + pltpu.roll(a, LANES - ROPE_DIM // 2, 1) * sa + pltpu.roll(a, ROPE_DIM // 2, 1) * sb
            o_ref[:, col:col + LANES] = a
            if kv_refs and col >= kv_col0:
                which, head = divmod((col - kv_col0) // LANES, kv_heads)
                kv_refs[which][pl.ds(head, tm, stride=kv_heads), :] = a


def norm_matmul(x, g, w, layer, *, tm, rope=None, rope_cols=0, kv_col0=None, kv_heads=0, name):
    m, d = x.shape
    n = w.shape[2]
    in_specs = [
        pl.BlockSpec((tm, d), lambda i: (i, 0)),
        pl.BlockSpec((1, d), lambda i: (0, 0)),
        pl.BlockSpec((None, d, n), lambda i: (layer, 0, 0)),
    ]
    args = [x, g.reshape(1, d), w]
    if rope_cols:
        n_tab = rope[0].shape[0] // tm
        for t in rope:
            in_specs.append(pl.BlockSpec((tm, LANES), lambda i: (i % n_tab, 0)))
            args.append(t)
    out_specs = [pl.BlockSpec((tm, n), lambda i: (i, 0))]
    out_shape = [jax.ShapeDtypeStruct((m, n), F32)]
    if kv_col0 is not None:
        out_specs += [pl.BlockSpec((tm * kv_heads, LANES), lambda i: (i, 0))] * 2
        out_shape += [jax.ShapeDtypeStruct((m * kv_heads, LANES), F32)] * 2
    out = pl.pallas_call(
        functools.partial(_norm_matmul_kernel, rope_cols=rope_cols, kv_col0=kv_col0, kv_heads=kv_heads),
        grid=(m // tm,),
        in_specs=in_specs,
        out_specs=out_specs,
        out_shape=out_shape,
        compiler_params=_params(1),
        name=name,
    )(*args)
    return out if kv_col0 is not None else out[0]


def _matmul_norm_res_kernel(a_ref, w_ref, g_ref, x_ref, o_ref):
    m = jnp.dot(a_ref[...].astype(BF16), w_ref[...], preferred_element_type=F32)
    o_ref[...] = x_ref[...] + _rms(m, g_ref[...])


def matmul_norm_res(a, w, layer, g, x, *, tm, name):
    m, k = a.shape
    n = w.shape[2]
    return pl.pallas_call(
        _matmul_norm_res_kernel,
        grid=(m // tm,),
        in_specs=[
            pl.BlockSpec((tm, k), lambda i: (i, 0)),
            pl.BlockSpec((None, k, n), lambda i: (layer, 0, 0)),
            pl.BlockSpec((1, n), lambda i: (0, 0)),
            pl.BlockSpec((tm, n), lambda i: (i, 0)),
        ],
        out_specs=pl.BlockSpec((tm, n), lambda i: (i, 0)),
        out_shape=jax.ShapeDtypeStruct((m, n), F32),
        compiler_params=_params(1),
        name=name,
    )(a, w, g.reshape(1, n), x)


def _ffn_kernel(x_ref, g2_ref, wg_ref, wv_ref, cwg_ref, cwv_ref, cbg_ref, cbv_ref, wd_ref, g3_ref,
                o_ref, cog_ref, cov_ref,
                h_ref, acc_ref, carg_ref, carv_ref, *buf_refs, tiles_per_seq):
    i = pl.program_id(0)
    j = pl.program_id(1)
    tm = x_ref.shape[0]
    n_chunks = len(buf_refs) // 2
    bufg_refs, bufv_refs = buf_refs[:n_chunks], buf_refs[n_chunks:]
    chunk = bufg_refs[0].shape[1]
    col = lambda c: slice(c * chunk, (c + 1) * chunk)

    @pl.when(j == 0)
    def _():
        h_ref[...] = _rms(x_ref[...], g2_ref[...]).astype(BF16)
        acc_ref[...] = jnp.zeros_like(acc_ref)

    seq_start = (i % tiles_per_seq) == 0

    @pl.when(seq_start)
    def _():
        for buf_ref in buf_refs:
            buf_ref[0:SUBLANES, :] = jnp.zeros((SUBLANES, chunk), F32)

    @pl.when(jnp.logical_not(seq_start))
    def _():
        for c in range(n_chunks):
            bufg_refs[c][0:SUBLANES, :] = carg_ref[j, :, col(c)]
            bufv_refs[c][0:SUBLANES, :] = carv_ref[j, :, col(c)]

    for c in range(n_chunks):
        bufg_refs[c][SUBLANES:SUBLANES + tm, :] = jnp.dot(h_ref[...], wg_ref[:, col(c)], preferred_element_type=F32)
        bufv_refs[c][SUBLANES:SUBLANES + tm, :] = jnp.dot(h_ref[...], wv_ref[:, col(c)], preferred_element_type=F32)

    def conv(c, cw_ref, cb_ref, buf_ref, car_ref, co_ref):
        tail = buf_ref[tm:tm + SUBLANES, :]
        car_ref[j, :, col(c)] = tail
        co_ref[0, :, col(c)] = tail
        cw = cw_ref[:, col(c)]
        y = cb_ref[:, col(c)] + cw[0:1] * buf_ref[SUBLANES - 2:SUBLANES - 2 + tm, :]
        y = y + cw[1:2] * buf_ref[SUBLANES - 1:SUBLANES - 1 + tm, :]
        return y + cw[2:3] * buf_ref[SUBLANES:SUBLANES + tm, :]

    for c in range(n_chunks):
        cg = conv(c, cwg_ref, cbg_ref, bufg_refs[c], carg_ref, cog_ref)
        cv = conv(c, cwv_ref, cbv_ref, bufv_refs[c], carv_ref, cov_ref)
        act = (jax.nn.gelu(cg, approximate=True) * cv).astype(BF16)
        acc_ref[...] += jnp.dot(act, wd_ref[col(c), :], preferred_element_type=F32)

    @pl.when(j == pl.num_programs(1) - 1)
    def _():
        o_ref[...] = x_ref[...] + _rms(acc_ref[...], g3_ref[...])


def ffn(x, g2, w_up, conv_w, conv_b, w_down, g3, layer, *, seq_len, tm, tf, chunk, name):
    m, d = x.shape
    d_ff = w_down.shape[1]
    nj = d_ff // tf
    tiles_per_seq = seq_len // tm
    conv_b2 = conv_b.reshape(1, 2 * d_ff)
    out, cog, cov = pl.pallas_call(
        functools.partial(_ffn_kernel, tiles_per_seq=tiles_per_seq),
        grid=(m // tm, nj),
        in_specs=[
            pl.BlockSpec((tm, d), lambda i, j: (i, 0)),
            pl.BlockSpec((1, d), lambda i, j: (0, 0)),
            pl.BlockSpec((None, d, tf), lambda i, j: (layer, 0, j)),
            pl.BlockSpec((None, d, tf), lambda i, j: (layer, 0, j + nj)),
            pl.BlockSpec((CONV_W, tf), lambda i, j: (0, j)),
            pl.BlockSpec((CONV_W, tf), lambda i, j: (0, j + nj)),
            pl.BlockSpec((1, tf), lambda i, j: (0, j)),
            pl.BlockSpec((1, tf), lambda i, j: (0, j + nj)),
            pl.BlockSpec((None, tf, d), lambda i, j: (layer, j, 0)),
            pl.BlockSpec((1, d), lambda i, j: (0, 0)),
        ],
        out_specs=[
            pl.BlockSpec((tm, d), lambda i, j: (i, 0)),
            pl.BlockSpec((1, SUBLANES, tf), lambda i, j: (i, 0, j)),
            pl.BlockSpec((1, SUBLANES, tf), lambda i, j: (i, 0, j)),
        ],
        out_shape=[
            jax.ShapeDtypeStruct((m, d), F32),
            jax.ShapeDtypeStruct((m // tm, SUBLANES, d_ff), F32),
            jax.ShapeDtypeStruct((m // tm, SUBLANES, d_ff), F32),
        ],
        scratch_shapes=[
            pltpu.VMEM((tm, d), BF16),
            pltpu.VMEM((tm, d), F32),
            pltpu.VMEM((nj, SUBLANES, tf), F32),
            pltpu.VMEM((nj, SUBLANES, tf), F32),
        ] + [pltpu.VMEM((tm + SUBLANES, chunk), F32)] * (2 * (tf // chunk)),
        compiler_params=_params(2),
        name=name,
    )(x, g2.reshape(1, d), w_up, w_up, conv_w, conv_w, conv_b2, conv_b2, w_down, g3.reshape(1, d))
    return out, cog, cov


def _ffn_step_kernel(x_ref, g2_ref, wg_ref, wv_ref, cwg_ref, cwv_ref, cbg_ref, cbv_ref,
                     p0g_ref, p0v_ref, p1g_ref, p1v_ref, wd_ref, g3_ref,
                     o_ref, ug_ref, uv_ref, h_ref, acc_ref):
    j = pl.program_id(0)

    @pl.when(j == 0)
    def _():
        h_ref[...] = _rms(x_ref[...], g2_ref[...]).astype(BF16)
        acc_ref[...] = jnp.zeros_like(acc_ref)

    h = h_ref[...]

    def conv(w_ref, cw_ref, cb_ref, p0_ref, p1_ref, u_ref):
        u = jnp.dot(h, w_ref[...], preferred_element_type=F32)
        u_ref[...] = u
        cw = cw_ref[...]
        c = cb_ref[...] + cw[0:1] * p0_ref[...]
        c = c + cw[1:2] * p1_ref[...]
        return c + cw[2:3] * u

    cg = conv(wg_ref, cwg_ref, cbg_ref, p0g_ref, p1g_ref, ug_ref)
    cv = conv(wv_ref, cwv_ref, cbv_ref, p0v_ref, p1v_ref, uv_ref)
    act = (jax.nn.gelu(cg, approximate=True) * cv).astype(BF16)
    acc_ref[...] += jnp.dot(act, wd_ref[...], preferred_element_type=F32)

    @pl.when(j == pl.num_programs(0) - 1)
    def _():
        o_ref[...] = x_ref[...] + _rms(acc_ref[...], g3_ref[...])


def ffn_step(x, g2, w_up, conv_w, conv_b, prev0, prev1, w_down, g3, layer, *, tf, name):
    m, d = x.shape
    d_ff = w_down.shape[1]
    nj = d_ff // tf
    conv_b2 = conv_b.reshape(1, 2 * d_ff)
    lo = lambda j: (0, j)
    hi = lambda j: (0, j + nj)
    out, ug, uv = pl.pallas_call(
        _ffn_step_kernel,
        grid=(nj,),
        in_specs=[
            pl.BlockSpec((m, d), lambda j: (0, 0)),
            pl.BlockSpec((1, d), lambda j: (0, 0)),
            pl.BlockSpec((None, d, tf), lambda j: (layer, 0, j)),
            pl.BlockSpec((None, d, tf), lambda j: (layer, 0, j + nj)),
            pl.BlockSpec((CONV_W, tf), lo),
            pl.BlockSpec((CONV_W, tf), hi),
            pl.BlockSpec((1, tf), lo),
            pl.BlockSpec((1, tf), hi),
            pl.BlockSpec((m, tf), lo),
            pl.BlockSpec((m, tf), hi),
            pl.BlockSpec((m, tf), lo),
            pl.BlockSpec((m, tf), hi),
            pl.BlockSpec((None, tf, d), lambda j: (layer, j, 0)),
            pl.BlockSpec((1, d), lambda j: (0, 0)),
        ],
        out_specs=[
            pl.BlockSpec((m, d), lambda j: (0, 0)),
            pl.BlockSpec((m, tf), lo),
            pl.BlockSpec((m, tf), lo),
        ],
        out_shape=[
            jax.ShapeDtypeStruct((m, d), F32),
            jax.ShapeDtypeStruct((m, d_ff), F32),
            jax.ShapeDtypeStruct((m, d_ff), F32),
        ],
        scratch_shapes=[pltpu.VMEM((m, d), BF16), pltpu.VMEM((m, d), F32)],
        compiler_params=_params(1),
        name=name,
    )(x, g2.reshape(1, d), w_up, w_up, conv_w, conv_w, conv_b2, conv_b2,
      prev0, prev0, prev1, prev1, w_down, g3.reshape(1, d))
    return out, jnp.concatenate([ug, uv], axis=-1)


def _suffix_sum_matrix():
    j = lax.broadcasted_iota(jnp.int32, (2 * LANES, 2 * LANES), 0) % LANES
    s = lax.broadcasted_iota(jnp.int32, (2 * LANES, 2 * LANES), 1)
    return jnp.where((s >= LANES) | (j > s), 1.0, 0.0).astype(BF16)


def _sb_terms(z, valid):
    sp = jnp.maximum(z, 0.0) + jnp.log(1.0 + jnp.exp(-jnp.abs(z)))
    log_beta = z - sp
    if valid is not None:
        sp = jnp.where(valid, sp, 0.0)
    return sp, log_beta


def _hi_lo(x):
    hi = x.astype(BF16)
    return jnp.concatenate([hi, (x - hi.astype(F32)).astype(BF16)], axis=1)


def _sb_weight(log_beta, cum, car, valid):
    w = jnp.exp(log_beta - cum[:, :LANES] - car)
    if valid is not None:
        w = jnp.where(valid, w, 0.0)
    return w.astype(BF16), car + cum[:, LANES:]


def _sb_prompt_kernel(bias_ref, q_ref, k_ref, v_ref, w2_ref, o_ref, acc_ref, car_ref, bias_v_ref, *,
                      scale, group, row_splits, sweep_blocks, q_blocks):
    kvh = pl.program_id(1)
    tq = q_ref.shape[0] // q_blocks
    rows = group * tq
    split = rows // row_splits
    w2 = w2_ref[...]
    for g in range(group):
        bias_v_ref[g * tq:(g + 1) * tq, :] = jnp.full((tq, LANES), bias_ref[kvh * group + g], F32)
    t_idx = lax.broadcasted_iota(jnp.int32, (split, tq), 0) % tq
    s_idx = lax.broadcasted_iota(jnp.int32, (split, tq), 1)
    rsl = lambda r: slice(r * split, (r + 1) * split)

    def sweep(qs, start, n_blocks, diagonal):
        top_valid = s_idx < jnp.where(diagonal, t_idx, tq)
        valid_of = lambda b: top_valid if b == n_blocks - 1 else None
        st = start if isinstance(start, int) else pl.multiple_of(start, tq)
        kb = k_ref[pl.ds(st, n_blocks * tq), :].astype(BF16)
        vb = v_ref[pl.ds(st, n_blocks * tq), :].astype(BF16)
        order = list(reversed(range(n_blocks)))
        zs = [lax.dot_general(qs[rsl(r)], kb, _NT, preferred_element_type=F32) for r in range(row_splits)]
        log_betas, cums = [], []
        for r in range(row_splits):
            bias = bias_v_ref[rsl(r)]
            sps, lbs = [], {}
            for b in order:
                sp, lbs[b] = _sb_terms(zs[r][:, b * tq:(b + 1) * tq] * scale + bias, valid_of(b))
                sps.append(_hi_lo(sp))
            log_betas.append(lbs)
            cums.append(jnp.dot(jnp.concatenate(sps, axis=0), w2, preferred_element_type=F32))
        for r in range(row_splits):
            car = car_ref[rsl(r)]
            ws = {}
            for t, b in enumerate(order):
                ws[b], car = _sb_weight(log_betas[r][b], cums[r][t * split:(t + 1) * split], car, valid_of(b))
            car_ref[rsl(r)] = car
            w_all = jnp.concatenate([ws[b] for b in range(n_blocks)], axis=1)
            acc_ref[rsl(r)] += jnp.dot(w_all, vb, preferred_element_type=F32)

    def query_block(u, carry):
        i = pl.program_id(2) * q_blocks + u
        row0 = pl.multiple_of(u * tq, tq)
        q = q_ref[pl.ds(row0, tq), :]
        qs = jnp.concatenate([q[:, g * HD_A:(g + 1) * HD_A] for g in range(group)], axis=0).astype(BF16)
        acc_ref[...] = jnp.zeros_like(acc_ref)
        car_ref[...] = jnp.zeros_like(car_ref)

        n_full = (i + 1) // sweep_blocks

        def full(p, c):
            sweep(qs, (i + 1 - sweep_blocks * (p + 1)) * tq, sweep_blocks, p == 0)
            return c

        lax.fori_loop(0, n_full, full, 0)
        for rem in range(1, sweep_blocks):
            @pl.when((i + 1) % sweep_blocks == rem)
            def _():
                sweep(qs, 0, rem, n_full == 0)

        for g in range(group):
            o_ref[pl.ds(row0, tq), g * HD_A:(g + 1) * HD_A] = acc_ref[g * tq:(g + 1) * tq].astype(o_ref.dtype)
        return carry

    lax.fori_loop(0, q_blocks, query_block, 0)


def sb_prompt(qkv, bias, *, n_seq, seq_len, q_blocks, name):
    m = qkv.shape[0]
    h_a = bias.shape[0]
    group = h_a // KVH_A
    tq = PAGE_SIZE * q_blocks
    nq = seq_len // tq
    qw = group * HD_A
    k_col0 = h_a
    v_col0 = k_col0 + KVH_A
    return pl.pallas_call(
        functools.partial(_sb_prompt_kernel, scale=HD_A ** -0.5, group=group, row_splits=2, sweep_blocks=4,
                          q_blocks=q_blocks),
        grid=(n_seq, KVH_A, nq),
        in_specs=[
            pl.BlockSpec(memory_space=pltpu.SMEM),
            pl.BlockSpec((tq, qw), lambda b, h, i: (b * nq + i, h)),
            pl.BlockSpec((seq_len, HD_A), lambda b, h, i: (b, k_col0 + h)),
            pl.BlockSpec((seq_len, HD_A), lambda b, h, i: (b, v_col0 + h)),
            pl.BlockSpec((2 * LANES, 2 * LANES), lambda b, h, i: (0, 0)),
        ],
        out_specs=pl.BlockSpec((tq, qw), lambda b, h, i: (b * nq + i, h)),
        out_shape=jax.ShapeDtypeStruct((m, h_a * HD_A), BF16),
        scratch_shapes=[pltpu.VMEM((group * PAGE_SIZE, HD_A), F32), pltpu.VMEM((group * PAGE_SIZE, LANES), F32),
                        pltpu.VMEM((group * PAGE_SIZE, LANES), F32)],
        compiler_params=_params(3),
        name=name,
    )(bias, qkv, qkv, qkv, _suffix_sum_matrix())


def _sb_decode_kernel(pt_ref, q_ref, bias_ref, w2_ref, *refs, scale, pps, group):
    k_refs = refs[:pps]
    v_refs = refs[pps:2 * pps]
    o_ref, acc_ref, car_ref = refs[2 * pps:]
    s = pl.program_id(1)

    @pl.when(s == 0)
    def _():
        acc_ref[...] = jnp.zeros_like(acc_ref)
        car_ref[...] = jnp.zeros_like(car_ref)

    q = q_ref[0].astype(BF16)
    n_chunks = k_refs[0].shape[0] // LANES
    row_kvh = lax.broadcasted_iota(jnp.int32, (q.shape[0], LANES), 0) // group
    col_kvh = lax.broadcasted_iota(jnp.int32, (q.shape[0], LANES), 1) % KVH_A
    own = row_kvh == col_kvh
    bias = bias_ref[...]
    terms = []
    for r in range(pps):
        kp = k_refs[r][...].astype(BF16)
        z = lax.dot_general(q, kp, _NT, preferred_element_type=F32) * scale
        for c in reversed(range(n_chunks)):
            terms.append(_sb_terms(z[:, c * LANES:(c + 1) * LANES] + bias, own))
    n_heads = q.shape[0]
    cum = jnp.dot(jnp.concatenate([_hi_lo(sp) for sp, _ in terms], axis=0), w2_ref[...], preferred_element_type=F32)
    car = car_ref[...]
    ws = []
    for t, (_, log_beta) in enumerate(terms):
        w, car = _sb_weight(log_beta, cum[t * n_heads:(t + 1) * n_heads], car, own)
        ws.append(w)
    car_ref[...] = car
    acc = acc_ref[...]
    for r in range(pps):
        w_page = jnp.concatenate([ws[r * n_chunks + (n_chunks - 1 - c)] for c in range(n_chunks)], axis=1)
        acc = acc + jnp.dot(w_page, v_refs[r][...].astype(BF16), preferred_element_type=F32)
    acc_ref[...] = acc

    @pl.when(s == pl.num_programs(1) - 1)
    def _():
        o_ref[0] = acc_ref[...]


def sb_decode(q, bias, cache_k, cache_v, page_table, *, layer, pps, name):
    n, h_a, _ = q.shape
    group = h_a // KVH_A
    n_pages = page_table.shape[1]
    page_rows = cache_k.shape[2]

    def page_spec(r):
        def index_map(b, s, pt):
            return (layer, pt[b * n_pages + (n_pages - 1 - (s * pps + r))], 0, 0)
        return pl.BlockSpec((None, None, page_rows, HD_A), index_map)

    grid_spec = pltpu.PrefetchScalarGridSpec(
        num_scalar_prefetch=1,
        grid=(n, n_pages // pps),
        in_specs=[
            pl.BlockSpec((1, h_a, HD_A), lambda b, s, pt: (b, 0, 0)),
            pl.BlockSpec((h_a, LANES), lambda b, s, pt: (0, 0)),
            pl.BlockSpec((2 * LANES, 2 * LANES), lambda b, s, pt: (0, 0)),
        ] + [page_spec(r) for r in range(pps)] * 2,
        out_specs=pl.BlockSpec((1, h_a, HD_A), lambda b, s, pt: (b, 0, 0)),
        scratch_shapes=[pltpu.VMEM((h_a, HD_A), F32), pltpu.VMEM((h_a, LANES), F32)],
    )
    bias_b = jnp.broadcast_to(bias.astype(F32)[:, None], (h_a, LANES))
    return pl.pallas_call(
        functools.partial(_sb_decode_kernel, scale=HD_A ** -0.5, pps=pps, group=group),
        grid_spec=grid_spec,
        out_shape=jax.ShapeDtypeStruct((n, h_a, HD_A), F32),
        compiler_params=_params(2),
        name=name,
    )(page_table.reshape(-1), q, bias_b, _suffix_sum_matrix(), *([cache_k] * pps), *([cache_v] * pps))


def _swa_prompt_kernel(sink_ref, q_ref, kc_ref, kp_ref, vc_ref, vp_ref, o_ref, *, scale, n_heads):
    i = pl.program_id(1)
    tq = q_ref.shape[0]
    half = HD_B
    lane = lax.broadcasted_iota(jnp.int32, (tq, LANES), 1)
    low = lane < half
    row = lax.broadcasted_iota(jnp.int32, (tq, tq), 0)
    col = lax.broadcasted_iota(jnp.int32, (tq, tq), 1)
    mask_cur = col <= row
    mask_prev = (col >= row) & (i > 0)
    mask = jnp.concatenate([mask_prev, mask_cur], axis=1)
    mask2 = jnp.concatenate([mask, mask], axis=0)
    low2 = jnp.concatenate([low, low], axis=0)
    group = n_heads // KVH_B
    for c in range(KVH_B // 2):
        kk = jnp.concatenate([kp_ref[:, c * LANES:(c + 1) * LANES], kc_ref[:, c * LANES:(c + 1) * LANES]], axis=0)
        vv = jnp.concatenate([vp_ref[:, c * LANES:(c + 1) * LANES], vc_ref[:, c * LANES:(c + 1) * LANES]], axis=0)
        kr = pltpu.roll(kk, half, 1)
        vr = pltpu.roll(vv, half, 1)
        for e in range(2):
            kvh = 2 * c + e
            kd = (jnp.where(low2, kk, kr) if e == 0 else jnp.where(low2, kr, kk)).astype(BF16)
            vd = (jnp.where(low2, vv, vr) if e == 0 else jnp.where(low2, vr, vv)).astype(BF16)
            for p in range(group // 2):
                h0 = kvh * group + 2 * p
                qcol = h0 // 2
                qp = q_ref[:, qcol * LANES:(qcol + 1) * LANES]
                q2 = jnp.concatenate([jnp.where(low, qp, 0.0), jnp.where(low, 0.0, qp)], axis=0).astype(BF16)
                s = lax.dot_general(q2, kd, _NT, preferred_element_type=F32) * scale
                s = jnp.where(mask2, s, -jnp.inf)
                rid = lax.broadcasted_iota(jnp.int32, (2 * tq, 1), 0)
                sink = jnp.where(rid < tq, sink_ref[h0], sink_ref[h0 + 1])
                m = jnp.maximum(jnp.max(s, axis=-1, keepdims=True), sink)
                pexp = jnp.exp(s - m)
                denom = jnp.sum(pexp, axis=-1, keepdims=True) + jnp.exp(sink - m)
                o2 = jnp.dot(pexp.astype(BF16), vd, preferred_element_type=F32) / denom
                o_ref[:, qcol * LANES:(qcol + 1) * LANES] = jnp.where(low, o2[:tq], o2[tq:]).astype(o_ref.dtype)


def swa_prompt(q, kv, sinks, *, n_seq, seq_len, name):
    m, qw = q.shape
    n_heads = sinks.shape[0]
    tq = WINDOW
    nq = seq_len // tq
    kw = KVH_B * HD_B
    cur = lambda col: (lambda b, i: (b * nq + i, col))
    prev = lambda col: (lambda b, i: (b * nq + jnp.maximum(i - 1, 0), col))
    return pl.pallas_call(
        functools.partial(_swa_prompt_kernel, scale=HD_B ** -0.5, n_heads=n_heads),
        grid=(n_seq, nq),
        in_specs=[
            pl.BlockSpec(memory_space=pltpu.SMEM),
            pl.BlockSpec((tq, qw), lambda b, i: (b * nq + i, 0)),
            pl.BlockSpec((tq, kw), cur(0)),
            pl.BlockSpec((tq, kw), prev(0)),
            pl.BlockSpec((tq, kw), cur(1)),
            pl.BlockSpec((tq, kw), prev(1)),
        ],
        out_specs=pl.BlockSpec((tq, qw), lambda b, i: (b * nq + i, 0)),
        out_shape=jax.ShapeDtypeStruct((m, qw), BF16),
        compiler_params=_params(2),
        name=name,
    )(sinks, q, kv, kv, kv, kv)


def _swa_decode_kernel(q_ref, sink_ref, kw_ref, vw_ref, kn_ref, vn_ref, o_ref, *, scale, group):
    q = q_ref[0]
    row_kvh = lax.broadcasted_iota(jnp.int32, q.shape, 0) // group
    lane_kvh = lax.broadcasted_iota(jnp.int32, q.shape, 1) // HD_B
    own = row_kvh == lane_kvh
    qbd = jnp.where(own, q, 0.0)
    qb = qbd.astype(BF16)
    kwin = kw_ref[0].astype(BF16)
    vwin = vw_ref[0].astype(BF16)
    kn = kn_ref[0]
    vn = vn_ref[0]
    s = lax.dot_general(qb, kwin, _NT, preferred_element_type=F32) * scale
    s_new = jnp.sum(qb.astype(F32) * kn.astype(BF16).astype(F32), axis=-1, keepdims=True) * scale
    sink = sink_ref[:, 0:1]
    m = jnp.maximum(jnp.maximum(jnp.max(s, axis=-1, keepdims=True), s_new), sink)
    p = jnp.exp(s - m)
    p_new = jnp.exp(s_new - m)
    denom = jnp.sum(p, axis=-1, keepdims=True) + p_new + jnp.exp(sink - m)
    o = jnp.dot(p.astype(BF16), vwin, preferred_element_type=F32)
    o = o + p_new.astype(BF16).astype(F32) * vn.astype(BF16).astype(F32)
    o = jnp.where(own, o / denom, 0.0)
    o = o + pltpu.roll(o, 2 * HD_B, 1)
    o = o + pltpu.roll(o, HD_B, 1)
    o_ref[0] = o[:, :HD_B]


def swa_decode(q, sinks, win_k, win_v, k_new, v_new, *, name):
    n, n_heads, _ = q.shape
    w = win_k.shape[1]
    kw = KVH_B * HD_B
    q_t = jnp.tile(q, (1, 1, KVH_B))
    sink_b = jnp.broadcast_to(sinks.astype(F32)[:, None], (n_heads, LANES))
    per_seq = lambda a, b: pl.BlockSpec((1, a, b), lambda s: (s, 0, 0))
    return pl.pallas_call(
        functools.partial(_swa_decode_kernel, scale=HD_B ** -0.5, group=n_heads // KVH_B),
        grid=(n,),
        in_specs=[
            per_seq(n_heads, kw),
            pl.BlockSpec((n_heads, LANES), lambda s: (0, 0)),
            per_seq(w, kw), per_seq(w, kw), per_seq(1, kw), per_seq(1, kw),
        ],
        out_specs=per_seq(n_heads, HD_B),
        out_shape=jax.ShapeDtypeStruct((n, n_heads, HD_B), F32),
        compiler_params=_params(1),
        name=name,
    )(q_t, sink_b, win_k, win_v, k_new, v_new)


def _rope_tables(pos):
    n = pos.shape[0]
    half = ROPE_DIM // 2
    inv_freq = 1.0 / (ROPE_THETA ** (jnp.arange(half, dtype=F32) * (2.0 / ROPE_DIM)))
    ang = pos.astype(F32)[:, None] * inv_freq[None, :]
    cos, sin = jnp.cos(ang), jnp.sin(ang)
    pad = lambda a, left, fill: jnp.concatenate(
        [jnp.full((n, left), fill, F32), a, jnp.full((n, HD_B - left - a.shape[1]), fill, F32)], axis=1)
    c = jnp.concatenate([cos, cos, jnp.ones((n, HD_B - ROPE_DIM), F32)], axis=1)
    sa = pad(-sin, 0, 0.0)
    sb = pad(sin, half, 0.0)
    two = lambda a: jnp.concatenate([a, a], axis=1)
    return two(c), two(sa), two(sb)


def kernel(x_prompt, x_sample, cache_a_k, cache_a_v, state_win_k, state_win_v, state_conv, page_table, norm_g, w_qkv_a, w_o_a, sb_bias, w_kv_b, kv_norm_g, w_q_b, w_o_b, sinks_b, w_up, conv_w, conv_b, w_down):
    n_seq, seq_len, d = x_prompt.shape
    n_dec = x_sample.shape[0]
    depth = norm_g.shape[0]
    n_a = w_qkv_a.shape[0]
    h_a = sb_bias.shape[1]
    h_b = sinks_b.shape[1]
    d_ff = w_down.shape[1]
    past_len = page_table.shape[1] * PAGE_SIZE
    qa_w = h_a * HD_A
    ka_w = KVH_A * HD_A
    kb_w = KVH_B * HD_B

    bf = lambda a: a.astype(BF16)
    w_qkv_a, w_o_a, w_kv_b, w_q_b, w_o_b, w_up, w_down = map(
        bf, (w_qkv_a, w_o_a, w_kv_b[None], w_q_b, w_o_b, w_up, w_down))

    rope_p = _rope_tables(jnp.arange(seq_len))
    rope_s = _rope_tables(jnp.full((n_dec,), past_len))
    cache_k = cache_a_k.reshape(cache_a_k.shape[:2] + (PAGE_SIZE * KVH_A, HD_A))
    cache_v = cache_a_v.reshape(cache_a_v.shape[:2] + (PAGE_SIZE * KVH_A, HD_A))
    win_k = state_win_k.reshape(n_dec, -1, kb_w)
    win_v = state_win_v.reshape(n_dec, -1, kb_w)

    x = x_prompt.reshape(n_seq * seq_len, d)
    ak_p, av_p, conv_p = [], [], []
    kv = None
    for l in range(depth):
        if l < n_a:
            qkv, k_rows, v_rows = norm_matmul(x, norm_g[l, 0], w_qkv_a, l, tm=QKV_TM, kv_col0=qa_w, kv_heads=KVH_A,
                                              name=f"p{l}_qkv")
            ak_p.append(k_rows.reshape(n_seq, seq_len, KVH_A, HD_A))
            av_p.append(v_rows.reshape(n_seq, seq_len, KVH_A, HD_A))
            o = sb_prompt(qkv, sb_bias[l], n_seq=n_seq, seq_len=seq_len, q_blocks=SB_Q_BLOCKS, name=f"p{l}_sb")
            x = matmul_norm_res(o, w_o_a, l, norm_g[l, 1], x, tm=WO_TM, name=f"p{l}_wo")
        else:
            jb = l - n_a
            if kv is None:
                kv = norm_matmul(x, kv_norm_g, w_kv_b, 0, tm=PROJ_TM, rope=rope_p, rope_cols=kb_w, name="p_kv")
            qb = norm_matmul(x, norm_g[l, 0], w_q_b, jb, tm=PROJ_TM, rope=rope_p, rope_cols=h_b * HD_B, name=f"p{l}_q")
            o = swa_prompt(qb, kv, sinks_b[jb], n_seq=n_seq, seq_len=seq_len, name=f"p{l}_swa")
            x = matmul_norm_res(o, w_o_b, jb, norm_g[l, 1], x, tm=WO_TM, name=f"p{l}_wo")
        x, cog, cov = ffn(x, norm_g[l, 2], w_up, conv_w[l], conv_b[l], w_down, norm_g[l, 3], l,
                          seq_len=seq_len, tm=FFN_TM, tf=FFN_TF, chunk=FFN_CHUNK, name=f"p{l}_ffn")
        last = slice(seq_len // FFN_TM - 1, None, seq_len // FFN_TM)
        conv_p.append(jnp.concatenate([cog[last, SUBLANES - 2:], cov[last, SUBLANES - 2:]], axis=-1))
    y_prompt = x.reshape(n_seq, seq_len, d)
    n_keep = min(WINDOW, seq_len)
    kv_tail = kv.reshape(n_seq, seq_len, 2 * kb_w)[:, seq_len - n_keep:]
    wk_p = kv_tail[..., :kb_w].reshape(n_seq, n_keep, KVH_B, HD_B)
    wv_p = kv_tail[..., kb_w:].reshape(n_seq, n_keep, KVH_B, HD_B)

    x = x_sample.reshape(n_dec, d)
    ak_s, av_s, conv_s = [], [], []
    kv = None
    for l in range(depth):
        if l < n_a:
            qkv, k_rows, v_rows = norm_matmul(x, norm_g[l, 0], w_qkv_a, l, tm=n_dec, kv_col0=qa_w, kv_heads=KVH_A,
                                              name=f"s{l}_qkv")
            ak_s.append(k_rows.reshape(n_dec, 1, KVH_A, HD_A))
            av_s.append(v_rows.reshape(n_dec, 1, KVH_A, HD_A))
            o = sb_decode(qkv[:, :qa_w].reshape(n_dec, h_a, HD_A), sb_bias[l], cache_k, cache_v, page_table,
                          layer=l, pps=DECODE_PAGES, name=f"s{l}_sb")
            x = matmul_norm_res(o.reshape(n_dec, qa_w), w_o_a, l, norm_g[l, 1], x, tm=n_dec, name=f"s{l}_wo")
        else:
            jb = l - n_a
            if kv is None:
                kv = norm_matmul(x, kv_norm_g, w_kv_b, 0, tm=n_dec, rope=rope_s, rope_cols=kb_w, name="s_kv")
                k_new = kv[:, None, :kb_w]
                v_new = kv[:, None, kb_w:]
            qb = norm_matmul(x, norm_g[l, 0], w_q_b, jb, tm=n_dec, rope=rope_s, rope_cols=h_b * HD_B, name=f"s{l}_q")
            o = swa_decode(qb.reshape(n_dec, h_b, HD_B), sinks_b[jb], win_k, win_v, k_new, v_new, name=f"s{l}_swa")
            x = matmul_norm_res(o.reshape(n_dec, h_b * HD_B), w_o_b, jb, norm_g[l, 1], x, tm=n_dec, name=f"s{l}_wo")
        x, u_new = ffn_step(x, norm_g[l, 2], w_up, conv_w[l], conv_b[l],
                            state_conv[l, :, 0], state_conv[l, :, 1], w_down, norm_g[l, 3], l,
                            tf=FFN_STEP_TF, name=f"s{l}_ffn")
        conv_s.append(jnp.concatenate([state_conv[l, :, 1:], u_new[:, None, :]], axis=1)[:, -(CONV_W - 1):])
    y_sample = x.reshape(n_dec, 1, d)
    n_keep_s = min(WINDOW, win_k.shape[1] + 1)
    wk_s = jnp.concatenate([win_k, k_new], axis=1)[:, -n_keep_s:].reshape(n_dec, n_keep_s, KVH_B, HD_B)
    wv_s = jnp.concatenate([win_v, v_new], axis=1)[:, -n_keep_s:].reshape(n_dec, n_keep_s, KVH_B, HD_B)

    return (y_prompt, y_sample, jnp.stack(ak_p), jnp.stack(av_p), jnp.stack(ak_s), jnp.stack(av_s),
            wk_p, wv_p, wk_s, wv_s, jnp.stack(conv_p), jnp.stack(conv_s))
```

```python
import functools

import jax
import jax.numpy as jnp
from jax import lax
from jax.experimental import pallas as pl
from jax.experimental.pallas import tpu as pltpu

F32 = jnp.float32
BF16 = jnp.bfloat16

EPS = 1e-6
PAGE_SIZE = 128
HD_A = 128
KVH_A = 4
HD_B = 64
KVH_B = 4
WINDOW = 128
ROPE_DIM = HD_B // 4
ROPE_THETA = 500000.0
CONV_W = 3

LANES = 128
SUBLANES = 8
VMEM_LIMIT = 56 * 1024 * 1024
QKV_TM = 256
PROJ_TM = 512
WO_TM = 512
FFN_TM = 512
FFN_TF = 512
FFN_CHUNK = 256
FFN_STEP_TF = 1408
SB_Q_BLOCKS = 4
DECODE_PAGES = 32

_NT = (((1,), (1,)), ((), ()))


def _params(n_axes, vmem=VMEM_LIMIT):
    return pltpu.CompilerParams(dimension_semantics=("arbitrary",) * n_axes, vmem_limit_bytes=vmem)


def _rms(x, g):
    ms = jnp.mean(x * x, axis=-1, keepdims=True)
    return x * lax.rsqrt(ms + EPS) * g


COL_CHUNK = 512


def _norm_matmul_kernel(*refs, rope_cols, kv_col0, kv_heads):
    refs = list(refs)
    x_ref, g_ref, w_ref = refs[:3]
    rope_refs = refs[3:6] if rope_cols else ()
    o_ref = refs[3 + len(rope_refs)]
    kv_refs = refs[4 + len(rope_refs):]
    tm, n = o_ref.shape
    h = _rms(x_ref[...], g_ref[...]).astype(BF16)
    if rope_cols:
        c, sa, sb = (r[...] for r in rope_refs)
    for c0 in range(0, n, COL_CHUNK):
        acc = jnp.dot(h, w_ref[:, c0:c0 + COL_CHUNK], preferred_element_type=F32)
        for t in range(0, min(COL_CHUNK, n - c0), LANES):
            a = acc[:, t:t + LANES]
            col = c0 + t
            if col < rope_cols:
                a = a * c + pltpu.roll(a, LANES - ROPE_DIM // 2, 1) * sa + pltpu.roll(a, ROPE_DIM // 2, 1) * sb
            o_ref[:, col:col + LANES] = a
            if kv_refs and col >= kv_col0:
                which, head = divmod((col - kv_col0) // LANES, kv_heads)
                kv_refs[which][pl.ds(head, tm, stride=kv_heads), :] = a


def norm_matmul(x, g, w, layer, *, tm, rope=None, rope_cols=0, kv_col0=None, kv_heads=0, name):
    m, d = x.shape
    n = w.shape[2]
    in_specs = [
        pl.BlockSpec((tm, d), lambda i: (i, 0)),
        pl.BlockSpec((1, d), lambda i: (0, 0)),
        pl.BlockSpec((None, d, n), lambda i: (layer, 0, 0)),
    ]
    args = [x, g.reshape(1, d), w]
    if rope_cols:
        n_tab = rope[0].shape[0] // tm
        for t in rope:
            in_specs.append(pl.BlockSpec((tm, LANES), lambda i: (i % n_tab, 0)))
            args.append(t)
    out_specs = [pl.BlockSpec((tm, n), lambda i: (i, 0))]
    out_shape = [jax.ShapeDtypeStruct((m, n), F32)]
    if kv_col0 is not None:
        out_specs += [pl.BlockSpec((tm * kv_heads, LANES), lambda i: (i, 0))] * 2
        out_shape += [jax.ShapeDtypeStruct((m * kv_heads, LANES), F32)] * 2
    out = pl.pallas_call(
        functools.partial(_norm_matmul_kernel, rope_cols=rope_cols, kv_col0=kv_col0, kv_heads=kv_heads),
        grid=(m // tm,),
        in_specs=in_specs,
        out_specs=out_specs,
        out_shape=out_shape,
        compiler_params=_params(1),
        name=name,
    )(*args)
    return out if kv_col0 is not None else out[0]


def _matmul_norm_res_kernel(a_ref, w_ref, g_ref, x_ref, o_ref):
    m = jnp.dot(a_ref[...].astype(BF16), w_ref[...], preferred_element_type=F32)
    o_ref[...] = x_ref[...] + _rms(m, g_ref[...])


def matmul_norm_res(a, w, layer, g, x, *, tm, name):
    m, k = a.shape
    n = w.shape[2]
    return pl.pallas_call(
        _matmul_norm_res_kernel,
        grid=(m // tm,),
        in_specs=[
            pl.BlockSpec((tm, k), lambda i: (i, 0)),
            pl.BlockSpec((None, k, n), lambda i: (layer, 0, 0)),
            pl.BlockSpec((1, n), lambda i: (0, 0)),
            pl.BlockSpec((tm, n), lambda i: (i, 0)),
        ],
        out_specs=pl.BlockSpec((tm, n), lambda i: (i, 0)),
        out_shape=jax.ShapeDtypeStruct((m, n), F32),
        compiler_params=_params(1),
        name=name,
    )(a, w, g.reshape(1, n), x)


def _ffn_kernel(x_ref, g2_ref, wg_ref, wv_ref, cwg_ref, cwv_ref, cbg_ref, cbv_ref, wd_ref, g3_ref,
                o_ref, cog_ref, cov_ref,
                h_ref, acc_ref, carg_ref, carv_ref, *buf_refs, tiles_per_seq):
    i = pl.program_id(0)
    j = pl.program_id(1)
    tm = x_ref.shape[0]
    n_chunks = len(buf_refs) // 2
    bufg_refs, bufv_refs = buf_refs[:n_chunks], buf_refs[n_chunks:]
    chunk = bufg_refs[0].shape[1]
    col = lambda c: slice(c * chunk, (c + 1) * chunk)

    @pl.when(j == 0)
    def _():
        h_ref[...] = _rms(x_ref[...], g2_ref[...]).astype(BF16)
        acc_ref[...] = jnp.zeros_like(acc_ref)

    seq_start = (i % tiles_per_seq) == 0

    @pl.when(seq_start)
    def _():
        for buf_ref in buf_refs:
            buf_ref[0:SUBLANES, :] = jnp.zeros((SUBLANES, chunk), F32)

    @pl.when(jnp.logical_not(seq_start))
    def _():
        for c in range(n_chunks):
            bufg_refs[c][0:SUBLANES, :] = carg_ref[j, :, col(c)]
            bufv_refs[c][0:SUBLANES, :] = carv_ref[j, :, col(c)]

    for c in range(n_chunks):
        bufg_refs[c][SUBLANES:SUBLANES + tm, :] = jnp.dot(h_ref[...], wg_ref[:, col(c)], preferred_element_type=F32)
        bufv_refs[c][SUBLANES:SUBLANES + tm, :] = jnp.dot(h_ref[...], wv_ref[:, col(c)], preferred_element_type=F32)

    def conv(c, cw_ref, cb_ref, buf_ref, car_ref, co_ref):
        tail = buf_ref[tm:tm + SUBLANES, :]
        car_ref[j, :, col(c)] = tail
        co_ref[0, :, col(c)] = tail
        cw = cw_ref[:, col(c)]
        y = cb_ref[:, col(c)] + cw[0:1] * buf_ref[SUBLANES - 2:SUBLANES - 2 + tm, :]
        y = y + cw[1:2] * buf_ref[SUBLANES - 1:SUBLANES - 1 + tm, :]
        return y + cw[2:3] * buf_ref[SUBLANES:SUBLANES + tm, :]

    for c in range(n_chunks):
        cg = conv(c, cwg_ref, cbg_ref, bufg_refs[c], carg_ref, cog_ref)
        cv = conv(c, cwv_ref, cbv_ref, bufv_refs[c], carv_ref, cov_ref)
        act = (jax.nn.gelu(cg, approximate=True) * cv).astype(BF16)
        acc_ref[...] += jnp.dot(act, wd_ref[col(c), :], preferred_element_type=F32)

    @pl.when(j == pl.num_programs(1) - 1)
    def _():
        o_ref[...] = x_ref[...] + _rms(acc_ref[...], g3_ref[...])


def ffn(x, g2, w_up, conv_w, conv_b, w_down, g3, layer, *, seq_len, tm, tf, chunk, name):
    m, d = x.shape
    d_ff = w_down.shape[1]
    nj = d_ff // tf
    tiles_per_seq = seq_len // tm
    conv_b2 = conv_b.reshape(1, 2 * d_ff)
    out, cog, cov = pl.pallas_call(
        functools.partial(_ffn_kernel, tiles_per_seq=tiles_per_seq),
        grid=(m // tm, nj),
        in_specs=[
            pl.BlockSpec((tm, d), lambda i, j: (i, 0)),
            pl.BlockSpec((1, d), lambda i, j: (0, 0)),
            pl.BlockSpec((None, d, tf), lambda i, j: (layer, 0, j)),
            pl.BlockSpec((None, d, tf), lambda i, j: (layer, 0, j + nj)),
            pl.BlockSpec((CONV_W, tf), lambda i, j: (0, j)),
            pl.BlockSpec((CONV_W, tf), lambda i, j: (0, j + nj)),
            pl.BlockSpec((1, tf), lambda i, j: (0, j)),
            pl.BlockSpec((1, tf), lambda i, j: (0, j + nj)),
            pl.BlockSpec((None, tf, d), lambda i, j: (layer, j, 0)),
            pl.BlockSpec((1, d), lambda i, j: (0, 0)),
        ],
        out_specs=[
            pl.BlockSpec((tm, d), lambda i, j: (i, 0)),
            pl.BlockSpec((1, SUBLANES, tf), lambda i, j: (i, 0, j)),
            pl.BlockSpec((1, SUBLANES, tf), lambda i, j: (i, 0, j)),
        ],
        out_shape=[
            jax.ShapeDtypeStruct((m, d), F32),
            jax.ShapeDtypeStruct((m // tm, SUBLANES, d_ff), F32),
            jax.ShapeDtypeStruct((m // tm, SUBLANES, d_ff), F32),
        ],
        scratch_shapes=[
            pltpu.VMEM((tm, d), BF16),
            pltpu.VMEM((tm, d), F32),
            pltpu.VMEM((nj, SUBLANES, tf), F32),
            pltpu.VMEM((nj, SUBLANES, tf), F32),
        ] + [pltpu.VMEM((tm + SUBLANES, chunk), F32)] * (2 * (tf // chunk)),
        compiler_params=_params(2),
        name=name,
    )(x, g2.reshape(1, d), w_up, w_up, conv_w, conv_w, conv_b2, conv_b2, w_down, g3.reshape(1, d))
    return out, cog, cov


def _ffn_step_kernel(x_ref, g2_ref, wg_ref, wv_ref, cwg_ref, cwv_ref, cbg_ref, cbv_ref,
                     p0g_ref, p0v_ref, p1g_ref, p1v_ref, wd_ref, g3_ref,
                     o_ref, ug_ref, uv_ref, h_ref, acc_ref):
    j = pl.program_id(0)

    @pl.when(j == 0)
    def _():
        h_ref[...] = _rms(x_ref[...], g2_ref[...]).astype(BF16)
        acc_ref[...] = jnp.zeros_like(acc_ref)

    h = h_ref[...]

    def conv(w_ref, cw_ref, cb_ref, p0_ref, p1_ref, u_ref):
        u = jnp.dot(h, w_ref[...], preferred_element_type=F32)
        u_ref[...] = u
        cw = cw_ref[...]
        c = cb_ref[...] + cw[0:1] * p0_ref[...]
        c = c + cw[1:2] * p1_ref[...]
        return c + cw[2:3] * u

    cg = conv(wg_ref, cwg_ref, cbg_ref, p0g_ref, p1g_ref, ug_ref)
    cv = conv(wv_ref, cwv_ref, cbv_ref, p0v_ref, p1v_ref, uv_ref)
    act = (jax.nn.gelu(cg, approximate=True) * cv).astype(BF16)
    acc_ref[...] += jnp.dot(act, wd_ref[...], preferred_element_type=F32)

    @pl.when(j == pl.num_programs(0) - 1)
    def _():
        o_ref[...] = x_ref[...] + _rms(acc_ref[...], g3_ref[...])


def ffn_step(x, g2, w_up, conv_w, conv_b, prev0, prev1, w_down, g3, layer, *, tf, name):
    m, d = x.shape
    d_ff = w_down.shape[1]
    nj = d_ff // tf
    conv_b2 = conv_b.reshape(1, 2 * d_ff)
    lo = lambda j: (0, j)
    hi = lambda j: (0, j + nj)
    out, ug, uv = pl.pallas_call(
        _ffn_step_kernel,
        grid=(nj,),
        in_specs=[
            pl.BlockSpec((m, d), lambda j: (0, 0)),
            pl.BlockSpec((1, d), lambda j: (0, 0)),
            pl.BlockSpec((None, d, tf), lambda j: (layer, 0, j)),
            pl.BlockSpec((None, d, tf), lambda j: (layer, 0, j + nj)),
            pl.BlockSpec((CONV_W, tf), lo),
            pl.BlockSpec((CONV_W, tf), hi),
            pl.BlockSpec((1, tf), lo),
            pl.BlockSpec((1, tf), hi),
            pl.BlockSpec((m, tf), lo),
            pl.BlockSpec((m, tf), hi),
            pl.BlockSpec((m, tf), lo),
            pl.BlockSpec((m, tf), hi),
            pl.BlockSpec((None, tf, d), lambda j: (layer, j, 0)),
            pl.BlockSpec((1, d), lambda j: (0, 0)),
        ],
        out_specs=[
            pl.BlockSpec((m, d), lambda j: (0, 0)),
            pl.BlockSpec((m, tf), lo),
            pl.BlockSpec((m, tf), lo),
        ],
        out_shape=[
            jax.ShapeDtypeStruct((m, d), F32),
            jax.ShapeDtypeStruct((m, d_ff), F32),
            jax.ShapeDtypeStruct((m, d_ff), F32),
        ],
        scratch_shapes=[pltpu.VMEM((m, d), BF16), pltpu.VMEM((m, d), F32)],
        compiler_params=_params(1),
        name=name,
    )(x, g2.reshape(1, d), w_up, w_up, conv_w, conv_w, conv_b2, conv_b2,
      prev0, prev0, prev1, prev1, w_down, g3.reshape(1, d))
    return out, jnp.concatenate([ug, uv], axis=-1)


def _suffix_sum_matrix():
    j = lax.broadcasted_iota(jnp.int32, (2 * LANES, 2 * LANES), 0) % LANES
    s = lax.broadcasted_iota(jnp.int32, (2 * LANES, 2 * LANES), 1)
    return jnp.where((s >= LANES) | (j > s), 1.0, 0.0).astype(BF16)


def _sb_terms(z, valid):
    sp = jnp.maximum(z, 0.0) + jnp.log(1.0 + jnp.exp(-jnp.abs(z)))
    log_beta = z - sp
    if valid is not None:
        sp = jnp.where(valid, sp, 0.0)
    return sp, log_beta


def _hi_lo(x):
    hi = x.astype(BF16)
    return jnp.concatenate([hi, (x - hi.astype(F32)).astype(BF16)], axis=1)


def _sb_weight(log_beta, cum, car, valid):
    w = jnp.exp(log_beta - cum[:, :LANES] - car)
    if valid is not None:
        w = jnp.where(valid, w, 0.0)
    return w.astype(BF16), car + cum[:, LANES:]


def _sb_prompt_kernel(bias_ref, q_ref, k_ref, v_ref, w2_ref, o_ref, acc_ref, car_ref, bias_v_ref, *,
                      scale, group, row_splits, sweep_blocks, q_blocks):
    kvh = pl.program_id(1)
    tq = q_ref.shape[0] // q_blocks
    rows = group * tq
    split = rows // row_splits
    w2 = w2_ref[...]
    for g in range(group):
        bias_v_ref[g * tq:(g + 1) * tq, :] = jnp.full((tq, LANES), bias_ref[kvh * group + g], F32)
    t_idx = lax.broadcasted_iota(jnp.int32, (split, tq), 0) % tq
    s_idx = lax.broadcasted_iota(jnp.int32, (split, tq), 1)
    rsl = lambda r: slice(r * split, (r + 1) * split)

    def sweep(qs, start, n_blocks, diagonal):
        top_valid = s_idx < jnp.where(diagonal, t_idx, tq)
        valid_of = lambda b: top_valid if b == n_blocks - 1 else None
        st = start if isinstance(start, int) else pl.multiple_of(start, tq)
        kb = k_ref[pl.ds(st, n_blocks * tq), :].astype(BF16)
        vb = v_ref[pl.ds(st, n_blocks * tq), :].astype(BF16)
        order = list(reversed(range(n_blocks)))
        zs = [lax.dot_general(qs[rsl(r)], kb, _NT, preferred_element_type=F32) for r in range(row_splits)]
        log_betas, cums = [], []
        for r in range(row_splits):
            bias = bias_v_ref[rsl(r)]
            sps, lbs = [], {}
            for b in order:
                sp, lbs[b] = _sb_terms(zs[r][:, b * tq:(b + 1) * tq] * scale + bias, valid_of(b))
                sps.append(_hi_lo(sp))
            log_betas.append(lbs)
            cums.append(jnp.dot(jnp.concatenate(sps, axis=0), w2, preferred_element_type=F32))
        for r in range(row_splits):
            car = car_ref[rsl(r)]
            ws = {}
            for t, b in enumerate(order):
                ws[b], car = _sb_weight(log_betas[r][b], cums[r][t * split:(t + 1) * split], car, valid_of(b))
            car_ref[rsl(r)] = car
            w_all = jnp.concatenate([ws[b] for b in range(n_blocks)], axis=1)
            acc_ref[rsl(r)] += jnp.dot(w_all, vb, preferred_element_type=F32)

    def query_block(u, carry):
        i = pl.program_id(2) * q_blocks + u
        row0 = pl.multiple_of(u * tq, tq)
        q = q_ref[pl.ds(row0, tq), :]
        qs = jnp.concatenate([q[:, g * HD_A:(g + 1) * HD_A] for g in range(group)], axis=0).astype(BF16)
        acc_ref[...] = jnp.zeros_like(acc_ref)
        car_ref[...] = jnp.zeros_like(car_ref)

        n_full = (i + 1) // sweep_blocks

        def full(p, c):
            sweep(qs, (i + 1 - sweep_blocks * (p + 1)) * tq, sweep_blocks, p == 0)
            return c

        lax.fori_loop(0, n_full, full, 0)
        for rem in range(1, sweep_blocks):
            @pl.when((i + 1) % sweep_blocks == rem)
            def _():
                sweep(qs, 0, rem, n_full == 0)

        for g in range(group):
            o_ref[pl.ds(row0, tq), g * HD_A:(g + 1) * HD_A] = acc_ref[g * tq:(g + 1) * tq].astype(o_ref.dtype)
        return carry

    lax.fori_loop(0, q_blocks, query_block, 0)


def sb_prompt(qkv, bias, *, n_seq, seq_len, q_blocks, name):
    m = qkv.shape[0]
    h_a = bias.shape[0]
    group = h_a // KVH_A
    tq = PAGE_SIZE * q_blocks
    nq = seq_len // tq
    qw = group * HD_A
    k_col0 = h_a
    v_col0 = k_col0 + KVH_A
    return pl.pallas_call(
        functools.partial(_sb_prompt_kernel, scale=HD_A ** -0.5, group=group, row_splits=2, sweep_blocks=8,
                          q_blocks=q_blocks),
        grid=(n_seq, KVH_A, nq),
        in_specs=[
            pl.BlockSpec(memory_space=pltpu.SMEM),
            pl.BlockSpec((tq, qw), lambda b, h, i: (b * nq + i, h)),
            pl.BlockSpec((seq_len, HD_A), lambda b, h, i: (b, k_col0 + h)),
            pl.BlockSpec((seq_len, HD_A), lambda b, h, i: (b, v_col0 + h)),
            pl.BlockSpec((2 * LANES, 2 * LANES), lambda b, h, i: (0, 0)),
        ],
        out_specs=pl.BlockSpec((tq, qw), lambda b, h, i: (b * nq + i, h)),
        out_shape=jax.ShapeDtypeStruct((m, h_a * HD_A), BF16),
        scratch_shapes=[pltpu.VMEM((group * PAGE_SIZE, HD_A), F32), pltpu.VMEM((group * PAGE_SIZE, LANES), F32),
                        pltpu.VMEM((group * PAGE_SIZE, LANES), F32)],
        compiler_params=_params(3),
        name=name,
    )(bias, qkv, qkv, qkv, _suffix_sum_matrix())


def _sb_decode_kernel(pt_ref, q_ref, bias_ref, w2_ref, *refs, scale, pps, group):
    k_refs = refs[:pps]
    v_refs = refs[pps:2 * pps]
    o_ref, acc_ref, car_ref = refs[2 * pps:]
    s = pl.program_id(1)

    @pl.when(s == 0)
    def _():
        acc_ref[...] = jnp.zeros_like(acc_ref)
        car_ref[...] = jnp.zeros_like(car_ref)

    q = q_ref[0].astype(BF16)
    n_chunks = k_refs[0].shape[0] // LANES
    row_kvh = lax.broadcasted_iota(jnp.int32, (q.shape[0], LANES), 0) // group
    col_kvh = lax.broadcasted_iota(jnp.int32, (q.shape[0], LANES), 1) % KVH_A
    own = row_kvh == col_kvh
    bias = bias_ref[...]
    terms = []
    for r in range(pps):
        kp = k_refs[r][...].astype(BF16)
        z = lax.dot_general(q, kp, _NT, preferred_element_type=F32) * scale
        for c in reversed(range(n_chunks)):
            terms.append(_sb_terms(z[:, c * LANES:(c + 1) * LANES] + bias, own))
    n_heads = q.shape[0]
    cum = jnp.dot(jnp.concatenate([_hi_lo(sp) for sp, _ in terms], axis=0), w2_ref[...], preferred_element_type=F32)
    car = car_ref[...]
    ws = []
    for t, (_, log_beta) in enumerate(terms):
        w, car = _sb_weight(log_beta, cum[t * n_heads:(t + 1) * n_heads], car, own)
        ws.append(w)
    car_ref[...] = car
    acc = acc_ref[...]
    for r in range(pps):
        w_page = jnp.concatenate([ws[r * n_chunks + (n_chunks - 1 - c)] for c in range(n_chunks)], axis=1)
        acc = acc + jnp.dot(w_page, v_refs[r][...].astype(BF16), preferred_element_type=F32)
    acc_ref[...] = acc

    @pl.when(s == pl.num_programs(1) - 1)
    def _():
        o_ref[0] = acc_ref[...]


def sb_decode(q, bias, cache_k, cache_v, page_table, *, layer, pps, name):
    n, h_a, _ = q.shape
    group = h_a // KVH_A
    n_pages = page_table.shape[1]
    page_rows = cache_k.shape[2]

    def page_spec(r):
        def index_map(b, s, pt):
            return (layer, pt[b * n_pages + (n_pages - 1 - (s * pps + r))], 0, 0)
        return pl.BlockSpec((None, None, page_rows, HD_A), index_map)

    grid_spec = pltpu.PrefetchScalarGridSpec(
        num_scalar_prefetch=1,
        grid=(n, n_pages // pps),
        in_specs=[
            pl.BlockSpec((1, h_a, HD_A), lambda b, s, pt: (b, 0, 0)),
            pl.BlockSpec((h_a, LANES), lambda b, s, pt: (0, 0)),
            pl.BlockSpec((2 * LANES, 2 * LANES), lambda b, s, pt: (0, 0)),
        ] + [page_spec(r) for r in range(pps)] * 2,
        out_specs=pl.BlockSpec((1, h_a, HD_A), lambda b, s, pt: (b, 0, 0)),
        scratch_shapes=[pltpu.VMEM((h_a, HD_A), F32), pltpu.VMEM((h_a, LANES), F32)],
    )
    bias_b = jnp.broadcast_to(bias.astype(F32)[:, None], (h_a, LANES))
    return pl.pallas_call(
        functools.partial(_sb_decode_kernel, scale=HD_A ** -0.5, pps=pps, group=group),
        grid_spec=grid_spec,
        out_shape=jax.ShapeDtypeStruct((n, h_a, HD_A), F32),
        compiler_params=_params(2),
        name=name,
    )(page_table.reshape(-1), q, bias_b, _suffix_sum_matrix(), *([cache_k] * pps), *([cache_v] * pps))


def _swa_prompt_kernel(sink_ref, q_ref, kc_ref, kp_ref, vc_ref, vp_ref, o_ref, *, scale, n_heads):
    i = pl.program_id(1)
    tq = q_ref.shape[0]
    half = HD_B
    lane = lax.broadcasted_iota(jnp.int32, (tq, LANES), 1)
    low = lane < half
    row = lax.broadcasted_iota(jnp.int32, (tq, tq), 0)
    col = lax.broadcasted_iota(jnp.int32, (tq, tq), 1)
    mask_cur = col <= row
    mask_prev = (col >= row) & (i > 0)
    mask = jnp.concatenate([mask_prev, mask_cur], axis=1)
    mask2 = jnp.concatenate([mask, mask], axis=0)
    low2 = jnp.concatenate([low, low], axis=0)
    group = n_heads // KVH_B
    for c in range(KVH_B // 2):
        kk = jnp.concatenate([kp_ref[:, c * LANES:(c + 1) * LANES], kc_ref[:, c * LANES:(c + 1) * LANES]], axis=0)
        vv = jnp.concatenate([vp_ref[:, c * LANES:(c + 1) * LANES], vc_ref[:, c * LANES:(c + 1) * LANES]], axis=0)
        kr = pltpu.roll(kk, half, 1)
        vr = pltpu.roll(vv, half, 1)
        for e in range(2):
            kvh = 2 * c + e
            kd = (jnp.where(low2, kk, kr) if e == 0 else jnp.where(low2, kr, kk)).astype(BF16)
            vd = (jnp.where(low2, vv, vr) if e == 0 else jnp.where(low2, vr, vv)).astype(BF16)
            for p in range(group // 2):
                h0 = kvh * group + 2 * p
                qcol = h0 // 2
                qp = q_ref[:, qcol * LANES:(qcol + 1) * LANES]
                q2 = jnp.concatenate([jnp.where(low, qp, 0.0), jnp.where(low, 0.0, qp)], axis=0).astype(BF16)
                s = lax.dot_general(q2, kd, _NT, preferred_element_type=F32) * scale
                s = jnp.where(mask2, s, -jnp.inf)
                rid = lax.broadcasted_iota(jnp.int32, (2 * tq, 1), 0)
                sink = jnp.where(rid < tq, sink_ref[h0], sink_ref[h0 + 1])
                m = jnp.maximum(jnp.max(s, axis=-1, keepdims=True), sink)
                pexp = jnp.exp(s - m)
                denom = jnp.sum(pexp, axis=-1, keepdims=True) + jnp.exp(sink - m)
                o2 = jnp.dot(pexp.astype(BF16), vd, preferred_element_type=F32) / denom
                o_ref[:, qcol * LANES:(qcol + 1) * LANES] = jnp.where(low, o2[:tq], o2[tq:]).astype(o_ref.dtype)


def swa_prompt(q, kv, sinks, *, n_seq, seq_len, name):
    m, qw = q.shape
    n_heads = sinks.shape[0]
    tq = WINDOW
    nq = seq_len // tq
    kw = KVH_B * HD_B
    cur = lambda col: (lambda b, i: (b * nq + i, col))
    prev = lambda col: (lambda b, i: (b * nq + jnp.maximum(i - 1, 0), col))
    return pl.pallas_call(
        functools.partial(_swa_prompt_kernel, scale=HD_B ** -0.5, n_heads=n_heads),
        grid=(n_seq, nq),
        in_specs=[
            pl.BlockSpec(memory_space=pltpu.SMEM),
            pl.BlockSpec((tq, qw), lambda b, i: (b * nq + i, 0)),
            pl.BlockSpec((tq, kw), cur(0)),
            pl.BlockSpec((tq, kw), prev(0)),
            pl.BlockSpec((tq, kw), cur(1)),
            pl.BlockSpec((tq, kw), prev(1)),
        ],
        out_specs=pl.BlockSpec((tq, qw), lambda b, i: (b * nq + i, 0)),
        out_shape=jax.ShapeDtypeStruct((m, qw), BF16),
        compiler_params=_params(2),
        name=name,
    )(sinks, q, kv, kv, kv, kv)


def _swa_decode_kernel(q_ref, sink_ref, kw_ref, vw_ref, kn_ref, vn_ref, o_ref, *, scale, group):
    q = q_ref[0]
    row_kvh = lax.broadcasted_iota(jnp.int32, q.shape, 0) // group
    lane_kvh = lax.broadcasted_iota(jnp.int32, q.shape, 1) // HD_B
    own = row_kvh == lane_kvh
    qbd = jnp.where(own, q, 0.0)
    qb = qbd.astype(BF16)
    kwin = kw_ref[0].astype(BF16)
    vwin = vw_ref[0].astype(BF16)
    kn = kn_ref[0]
    vn = vn_ref[0]
    s = lax.dot_general(qb, kwin, _NT, preferred_element_type=F32) * scale
    s_new = jnp.sum(qb.astype(F32) * kn.astype(BF16).astype(F32), axis=-1, keepdims=True) * scale
    sink = sink_ref[:, 0:1]
    m = jnp.maximum(jnp.maximum(jnp.max(s, axis=-1, keepdims=True), s_new), sink)
    p = jnp.exp(s - m)
    p_new = jnp.exp(s_new - m)
    denom = jnp.sum(p, axis=-1, keepdims=True) + p_new + jnp.exp(sink - m)
    o = jnp.dot(p.astype(BF16), vwin, preferred_element_type=F32)
    o = o + p_new.astype(BF16).astype(F32) * vn.astype(BF16).astype(F32)
    o = jnp.where(own, o / denom, 0.0)
    o = o + pltpu.roll(o, 2 * HD_B, 1)
    o = o + pltpu.roll(o, HD_B, 1)
    o_ref[0] = o[:, :HD_B]


def swa_decode(q, sinks, win_k, win_v, k_new, v_new, *, name):
    n, n_heads, _ = q.shape
    w = win_k.shape[1]
    kw = KVH_B * HD_B
    q_t = jnp.tile(q, (1, 1, KVH_B))
    sink_b = jnp.broadcast_to(sinks.astype(F32)[:, None], (n_heads, LANES))
    per_seq = lambda a, b: pl.BlockSpec((1, a, b), lambda s: (s, 0, 0))
    return pl.pallas_call(
        functools.partial(_swa_decode_kernel, scale=HD_B ** -0.5, group=n_heads // KVH_B),
        grid=(n,),
        in_specs=[
            per_seq(n_heads, kw),
            pl.BlockSpec((n_heads, LANES), lambda s: (0, 0)),
            per_seq(w, kw), per_seq(w, kw), per_seq(1, kw), per_seq(1, kw),
        ],
        out_specs=per_seq(n_heads, HD_B),
        out_shape=jax.ShapeDtypeStruct((n, n_heads, HD_B), F32),
        compiler_params=_params(1),
        name=name,
    )(q_t, sink_b, win_k, win_v, k_new, v_new)


def _rope_tables(pos):
    n = pos.shape[0]
    half = ROPE_DIM // 2
    inv_freq = 1.0 / (ROPE_THETA ** (jnp.arange(half, dtype=F32) * (2.0 / ROPE_DIM)))
    ang = pos.astype(F32)[:, None] * inv_freq[None, :]
    cos, sin = jnp.cos(ang), jnp.sin(ang)
    pad = lambda a, left, fill: jnp.concatenate(
        [jnp.full((n, left), fill, F32), a, jnp.full((n, HD_B - left - a.shape[1]), fill, F32)], axis=1)
    c = jnp.concatenate([cos, cos, jnp.ones((n, HD_B - ROPE_DIM), F32)], axis=1)
    sa = pad(-sin, 0, 0.0)
    sb = pad(sin, half, 0.0)
    two = lambda a: jnp.concatenate([a, a], axis=1)
    return two(c), two(sa), two(sb)


def kernel(x_prompt, x_sample, cache_a_k, cache_a_v, state_win_k, state_win_v, state_conv, page_table, norm_g, w_qkv_a, w_o_a, sb_bias, w_kv_b, kv_norm_g, w_q_b, w_o_b, sinks_b, w_up, conv_w, conv_b, w_down):
    n_seq, seq_len, d = x_prompt.shape
    n_dec = x_sample.shape[0]
    depth = norm_g.shape[0]
    n_a = w_qkv_a.shape[0]
    h_a = sb_bias.shape[1]
    h_b = sinks_b.shape[1]
    d_ff = w_down.shape[1]
    past_len = page_table.shape[1] * PAGE_SIZE
    qa_w = h_a * HD_A
    ka_w = KVH_A * HD_A
    kb_w = KVH_B * HD_B

    bf = lambda a: a.astype(BF16)
    w_qkv_a, w_o_a, w_kv_b, w_q_b, w_o_b, w_up, w_down = map(
        bf, (w_qkv_a, w_o_a, w_kv_b[None], w_q_b, w_o_b, w_up, w_down))

    rope_p = _rope_tables(jnp.arange(seq_len))
    rope_s = _rope_tables(jnp.full((n_dec,), past_len))
    cache_k = cache_a_k.reshape(cache_a_k.shape[:2] + (PAGE_SIZE * KVH_A, HD_A))
    cache_v = cache_a_v.reshape(cache_a_v.shape[:2] + (PAGE_SIZE * KVH_A, HD_A))
    win_k = state_win_k.reshape(n_dec, -1, kb_w)
    win_v = state_win_v.reshape(n_dec, -1, kb_w)

    x = x_prompt.reshape(n_seq * seq_len, d)
    ak_p, av_p, conv_p = [], [], []
    kv = None
    for l in range(depth):
        if l < n_a:
            qkv, k_rows, v_rows = norm_matmul(x, norm_g[l, 0], w_qkv_a, l, tm=QKV_TM, kv_col0=qa_w, kv_heads=KVH_A,
                                              name=f"p{l}_qkv")
            ak_p.append(k_rows.reshape(n_seq, seq_len, KVH_A, HD_A))
            av_p.append(v_rows.reshape(n_seq, seq_len, KVH_A, HD_A))
            o = sb_prompt(qkv, sb_bias[l], n_seq=n_seq, seq_len=seq_len, q_blocks=SB_Q_BLOCKS, name=f"p{l}_sb")
            x = matmul_norm_res(o, w_o_a, l, norm_g[l, 1], x, tm=WO_TM, name=f"p{l}_wo")
        else:
            jb = l - n_a
            if kv is None:
                kv = norm_matmul(x, kv_norm_g, w_kv_b, 0, tm=PROJ_TM, rope=rope_p, rope_cols=kb_w, name="p_kv")
            qb = norm_matmul(x, norm_g[l, 0], w_q_b, jb, tm=PROJ_TM, rope=rope_p, rope_cols=h_b * HD_B, name=f"p{l}_q")
            o = swa_prompt(qb, kv, sinks_b[jb], n_seq=n_seq, seq_len=seq_len, name=f"p{l}_swa")
            x = matmul_norm_res(o, w_o_b, jb, norm_g[l, 1], x, tm=WO_TM, name=f"p{l}_wo")
        x, cog, cov = ffn(x, norm_g[l, 2], w_up, conv_w[l], conv_b[l], w_down, norm_g[l, 3], l,
                          seq_len=seq_len, tm=FFN_TM, tf=FFN_TF, chunk=FFN_CHUNK, name=f"p{l}_ffn")
        last = slice(seq_len // FFN_TM - 1, None, seq_len // FFN_TM)
        conv_p.append(jnp.concatenate([cog[last, SUBLANES - 2:], cov[last, SUBLANES - 2:]], axis=-1))
    y_prompt = x.reshape(n_seq, seq_len, d)
    n_keep = min(WINDOW, seq_len)
    kv_tail = kv.reshape(n_seq, seq_len, 2 * kb_w)[:, seq_len - n_keep:]
    wk_p = kv_tail[..., :kb_w].reshape(n_seq, n_keep, KVH_B, HD_B)
    wv_p = kv_tail[..., kb_w:].reshape(n_seq, n_keep, KVH_B, HD_B)

    x = x_sample.reshape(n_dec, d)
    ak_s, av_s, conv_s = [], [], []
    kv = None
    for l in range(depth):
        if l < n_a:
            qkv, k_rows, v_rows = norm_matmul(x, norm_g[l, 0], w_qkv_a, l, tm=n_dec, kv_col0=qa_w, kv_heads=KVH_A,
                                              name=f"s{l}_qkv")
            ak_s.append(k_rows.reshape(n_dec, 1, KVH_A, HD_A))
            av_s.append(v_rows.reshape(n_dec, 1, KVH_A, HD_A))
            o = sb_decode(qkv[:, :qa_w].reshape(n_dec, h_a, HD_A), sb_bias[l], cache_k, cache_v, page_table,
                          layer=l, pps=DECODE_PAGES, name=f"s{l}_sb")
            x = matmul_norm_res(o.reshape(n_dec, qa_w), w_o_a, l, norm_g[l, 1], x, tm=n_dec, name=f"s{l}_wo")
        else:
            jb = l - n_a
            if kv is None:
                kv = norm_matmul(x, kv_norm_g, w_kv_b, 0, tm=n_dec, rope=rope_s, rope_cols=kb_w, name="s_kv")
                k_new = kv[:, None, :kb_w]
                v_new = kv[:, None, kb_w:]
            qb = norm_matmul(x, norm_g[l, 0], w_q_b, jb, tm=n_dec, rope=rope_s, rope_cols=h_b * HD_B, name=f"s{l}_q")
            o = swa_decode(qb.reshape(n_dec, h_b, HD_B), sinks_b[jb], win_k, win_v, k_new, v_new, name=f"s{l}_swa")
            x = matmul_norm_res(o.reshape(n_dec, h_b * HD_B), w_o_b, jb, norm_g[l, 1], x, tm=n_dec, name=f"s{l}_wo")
        x, u_new = ffn_step(x, norm_g[l, 2], w_up, conv_w[l], conv_b[l],
                            state_conv[l, :, 0], state_conv[l, :, 1], w_down, norm_g[l, 3], l,
                            tf=FFN_STEP_TF, name=f"s{l}_ffn")
        conv_s.append(jnp.concatenate([state_conv[l, :, 1:], u_new[:, None, :]], axis=1)[:, -(CONV_W - 1):])
    y_sample = x.reshape(n_dec, 1, d)
    n_keep_s = min(WINDOW, win_k.shape[1] + 1)
    wk_s = jnp.concatenate([win_k, k_new], axis=1)[:, -n_keep_s:].reshape(n_dec, n_keep_s, KVH_B, HD_B)
    wv_s = jnp.concatenate([win_v, v_new], axis=1)[:, -n_keep_s:].reshape(n_dec, n_keep_s, KVH_B, HD_B)

    return (y_prompt, y_sample, jnp.stack(ak_p), jnp.stack(av_p), jnp.stack(ak_s), jnp.stack(av_s),
            wk_p, wv_p, wk_s, wv_s, jnp.stack(conv_p), jnp.stack(conv_s))
```
